```python
import math
import jax, jax.numpy as jnp
from jax import lax
import numpy as np

D_MODEL = 2048
BATCH = 8
SEQ = 2048
DEPTH = 2

MEM_LEN = 256
NORM_EPS = 1e-6
RWKV_WIDTH = 1024
RWKV_HEAD = 64
RWKV_HEADS = RWKV_WIDTH // RWKV_HEAD
DECAY_RANK = 64
ICLR_RANK = 64
GN_EPS = 64e-5
SWA_HEAD = 64
SWA_Q_HEADS = 16
SWA_KV_HEADS = 2
SWA_WIDTH = SWA_Q_HEADS * SWA_HEAD
WINDOW = 128
BLOCK = 128
XA_HEADS = 4
XA_HEAD = 256
XA_WIDTH = XA_HEADS * XA_HEAD
N_BRANCH = 3

SHIFT_COLS = 3 * RWKV_WIDTH + DECAY_RANK + ICLR_RANK
COL_SIZES = (
    SHIFT_COLS,
    RWKV_WIDTH,
    SWA_WIDTH,
    2 * SWA_KV_HEADS * SWA_HEAD,
    SWA_WIDTH,
    XA_WIDTH,
    XA_WIDTH,
    N_BRANCH * D_MODEL,
)
D_IN = SHIFT_COLS + RWKV_WIDTH + 2 * SWA_WIDTH + 2 * SWA_KV_HEADS * SWA_HEAD + 2 * XA_WIDTH + N_BRANCH * D_MODEL

kernel_name = "hybrid_rwkv7_swa_sink_memxattn_gated"


def _split(p, sizes):
    idx = np.cumsum(np.array(sizes))[:-1].tolist()
    return jnp.split(p, idx, axis=-1)


def rmsnorm(x, g):
    xf = x.astype(jnp.float32)
    y = xf * lax.rsqrt(jnp.mean(xf * xf, axis=-1, keepdims=True) + NORM_EPS)
    return (y * g.astype(jnp.float32)).astype(x.dtype)


def rwkv7_time_mix(p, mu, decay_base, decay_up, iclr_base, iclr_up, k_k, k_a, r_k, gn_w, gn_b):
    B, T, _ = p.shape
    f32 = jnp.float32
    H, N = RWKV_HEADS, RWKV_HEAD
    prev = jnp.pad(p, ((0, 0), (1, 0), (0, 0)))[:, :-1]
    ps = p + mu * (prev - p)
    r, k, v, wd, ad = _split(ps, (RWKV_WIDTH, RWKV_WIDTH, RWKV_WIDTH, DECAY_RANK, ICLR_RANK))
    w_log = -jax.nn.softplus(-(decay_base + jnp.tanh(wd) @ decay_up).astype(f32)) - 0.5
    decay = jnp.exp(-jnp.exp(w_log))
    a = jax.nn.sigmoid((iclr_base + ad @ iclr_up).astype(f32))
    hs = lambda t: t.astype(f32).reshape(B, T, H, N)
    r, k, v, a, decay = hs(r), hs(k), hs(v), hs(a), hs(decay)
    kk = k * k_k.astype(f32).reshape(H, N)
    kk = kk / jnp.maximum(jnp.sqrt(jnp.sum(kk * kk, axis=-1, keepdims=True)), 1e-12)
    k = k * (1.0 + (a - 1.0) * k_a.astype(f32).reshape(H, N))

    def step(S, inp):
        r_t, w_t, k_t, v_t, kk_t, a_t = inp
        sa = jnp.einsum('bhvk,bhk->bhv', S, -kk_t)
        S = (S * w_t[:, :, None, :] + sa[..., None] * (kk_t * a_t)[:, :, None, :]
             + v_t[..., None] * k_t[:, :, None, :])
        return S, jnp.einsum('bhvk,bhk->bhv', S, r_t)

    seq = tuple(jnp.moveaxis(t, 1, 0) for t in (r, decay, k, v, kk, a))
    S0 = jnp.zeros((B, H, N, N), f32)
    _, y = lax.scan(step, S0, seq)
    y = jnp.moveaxis(y, 0, 1)
    mean = jnp.mean(y, axis=-1, keepdims=True)
    var = jnp.mean(jnp.square(y - mean), axis=-1, keepdims=True)
    y = ((y - mean) * lax.rsqrt(var + GN_EPS)).reshape(B, T, RWKV_WIDTH)
    y = y * gn_w.astype(f32) + gn_b.astype(f32)
    bonus = jnp.sum(r * k * r_k.astype(f32), axis=-1, keepdims=True) * v
    return (y + bonus.reshape(B, T, RWKV_WIDTH)).astype(p.dtype)


def sliding_window_gqa_sinks(q, kv, sinks):
    B, T, _ = q.shape
    n = T // BLOCK
    G = SWA_Q_HEADS // SWA_KV_HEADS
    f32 = jnp.float32
    q = q.reshape(B, n, BLOCK, SWA_KV_HEADS, G, SWA_HEAD)
    k, v = jnp.split(kv, 2, axis=-1)
    k = k.reshape(B, n, BLOCK, SWA_KV_HEADS, SWA_HEAD)
    v = v.reshape(B, n, BLOCK, SWA_KV_HEADS, SWA_HEAD)

    def band(t):
        prev = jnp.pad(t, ((0, 0), (1, 0), (0, 0), (0, 0), (0, 0)))[:, :-1]
        return jnp.concatenate([prev, t], axis=2)

    kw, vw = band(k), band(v)
    s = jnp.einsum('bnqhgd,bnkhd->bnhgqk', q, kw).astype(f32) * (SWA_HEAD ** -0.5)
    blk = jnp.arange(n)[:, None, None]
    qpos = blk * BLOCK + jnp.arange(BLOCK)[None, :, None]
    kpos = (blk - 1) * BLOCK + jnp.arange(2 * BLOCK)[None, None, :]
    rel = qpos - kpos
    mask = (rel >= 0) & (rel < WINDOW) & (kpos >= 0)
    s = jnp.where(mask[None, :, None, None], s, -jnp.inf)
    sink = sinks.astype(f32).reshape(SWA_KV_HEADS, G)[None, None, :, :, None, None]
    m = jnp.maximum(jnp.max(s, axis=-1, keepdims=True), sink)
    e = jnp.exp(s - m)
    p = e / (jnp.sum(e, axis=-1, keepdims=True) + jnp.exp(sink - m))
    o = jnp.einsum('bnhgqk,bnkhd->bnqhgd', p.astype(vw.dtype), vw)
    return o.reshape(B, T, SWA_WIDTH)


def memory_cross_attention(q, mem_n, w_mem_kv):
    B, T, _ = q.shape
    M = mem_n.shape[1]
    q = q.reshape(B, T, XA_HEADS, XA_HEAD)
    k, v = jnp.split(mem_n @ w_mem_kv, 2, axis=-1)
    k = k.reshape(B, M, XA_HEADS, XA_HEAD)
    v = v.reshape(B, M, XA_HEADS, XA_HEAD)
    s = jnp.einsum('bthd,bmhd->bhtm', q, k).astype(jnp.float32) * (XA_HEAD ** -0.5)
    p = jax.nn.softmax(s, axis=-1)
    o = jnp.einsum('bhtm,bmhd->bthd', p.astype(v.dtype), v)
    return o.reshape(B, T, XA_WIDTH)


def setup_inputs(seed: int = 0) -> dict:
    key = jax.random.key(seed)
    ks = jax.random.split(key, 24)
    f32 = jnp.float32
    nrm = lambda k, shape, s: jax.random.normal(k, shape, f32) * s
    L, D = DEPTH, D_MODEL
    return {
        "x": nrm(ks[0], (BATCH, SEQ, D), 1.0),
        "mem": nrm(ks[1], (BATCH, MEM_LEN, D), 1.0),
        "g_pre": 1.0 + nrm(ks[2], (L, D), 0.02),
        "w_in": nrm(ks[3], (L, D, D_IN), D ** -0.5),
        "mu_shift": jax.random.uniform(ks[4], (L, SHIFT_COLS), f32, 0.0, 1.0),
        "decay_base": jax.random.uniform(ks[5], (L, RWKV_WIDTH), f32, -6.0, 1.0),
        "decay_up": nrm(ks[6], (L, DECAY_RANK, RWKV_WIDTH), 0.1),
        "iclr_base": nrm(ks[7], (L, RWKV_WIDTH), 0.1),
        "iclr_up": nrm(ks[8], (L, ICLR_RANK, RWKV_WIDTH), 0.1),
        "k_k": 0.85 + nrm(ks[9], (L, RWKV_WIDTH), 0.02),
        "k_a": 1.0 + nrm(ks[10], (L, RWKV_WIDTH), 0.02),
        "r_k": nrm(ks[11], (L, RWKV_HEADS, RWKV_HEAD), 0.1),
        "gn_w": 1.0 + nrm(ks[12], (L, RWKV_WIDTH), 0.02),
        "gn_b": nrm(ks[13], (L, RWKV_WIDTH), 0.02),
        "attn_sinks": nrm(ks[14], (L, SWA_Q_HEADS), 0.5),
        "g_mem": 1.0 + nrm(ks[15], (L, D), 0.02),
        "w_mem_kv": nrm(ks[16], (L, D, 2 * XA_WIDTH), D ** -0.5),
        "w_up_rwkv": nrm(ks[17], (L, RWKV_WIDTH, D), RWKV_WIDTH ** -0.5),
        "w_up_swa": nrm(ks[18], (L, SWA_WIDTH, D), SWA_WIDTH ** -0.5),
        "w_up_xattn": nrm(ks[19], (L, XA_WIDTH, D), XA_WIDTH ** -0.5),
        "w_out": nrm(ks[20], (L, D, D), D ** -0.5),
        "g_post": 1.0 + nrm(ks[21], (L, D), 0.02),
    }


def reference(x, mem, g_pre, w_in, mu_shift, decay_base, decay_up, iclr_base, iclr_up,
              k_k, k_a, r_k, gn_w, gn_b, attn_sinks, g_mem, w_mem_kv,
              w_up_rwkv, w_up_swa, w_up_xattn, w_out, g_post):
    B, T, D = x.shape
    for l in range(DEPTH):
        h = rmsnorm(x, g_pre[l])
        p = h @ w_in[l]
        (c_rwkv, c_rwkv_gate, c_swa_q, c_swa_kv, c_swa_gate,
         c_xa_q, c_xa_gate, c_merge) = _split(p, COL_SIZES)
        y_a = rwkv7_time_mix(c_rwkv, mu_shift[l], decay_base[l], decay_up[l], iclr_base[l],
                             iclr_up[l], k_k[l], k_a[l], r_k[l], gn_w[l], gn_b[l])
        y_a = y_a * jax.nn.silu(c_rwkv_gate)
        y_b = sliding_window_gqa_sinks(c_swa_q, c_swa_kv, attn_sinks[l]) * jax.nn.silu(c_swa_gate)
        mem_n = rmsnorm(mem, g_mem[l])
        y_c = memory_cross_attention(c_xa_q, mem_n, w_mem_kv[l]) * jax.nn.silu(c_xa_gate)
        gates = jax.nn.sigmoid(c_merge.astype(jnp.float32)).astype(x.dtype).reshape(B, T, N_BRANCH, D)
        merged = (gates[:, :, 0] * (y_a @ w_up_rwkv[l])
                  + gates[:, :, 1] * (y_b @ w_up_swa[l])
                  + gates[:, :, 2] * (y_c @ w_up_xattn[l]))
        o = merged @ w_out[l]
        x = x + rmsnorm(o, g_post[l])
    return x
```

```python
import functools

import jax
import jax.numpy as jnp
import numpy as np
from jax import lax
from jax.experimental import pallas as pl
from jax.experimental.pallas import tpu as pltpu

F32 = jnp.float32
BF16 = jnp.bfloat16

D_MODEL = 2048
DEPTH = 2
MEM_LEN = 256
NORM_EPS = 1e-6
RWKV_WIDTH = 1024
RWKV_HEAD = 64
DECAY_RANK = 64
ICLR_RANK = 64
GN_EPS = 64e-5
SWA_HEAD = 64
SWA_Q_HEADS = 16
SWA_KV_HEADS = 2
SWA_WIDTH = SWA_Q_HEADS * SWA_HEAD
BLOCK = 128
XA_HEADS = 4
XA_HEAD = 256
XA_WIDTH = XA_HEADS * XA_HEAD
N_BRANCH = 3
SHIFT_COLS = 3 * RWKV_WIDTH + DECAY_RANK + ICLR_RANK

LANES = 128
CHUNK = 64
PAIR = 2 * RWKV_HEAD
VMEM_LIMIT = 56 * 1024 * 1024

_NT = (((1,), (1,)), ((), ()))
_TN = (((0,), (0,)), ((), ()))


def _params(*sem):
    return pltpu.CompilerParams(dimension_semantics=sem, vmem_limit_bytes=VMEM_LIMIT)


def _sigmoid(x):
    return 1.0 / (1.0 + jnp.exp(-x))


def _split3(x):
    hi = x.astype(BF16)
    r1 = x - hi.astype(F32)
    mid = r1.astype(BF16)
    lo = (r1 - mid.astype(F32)).astype(BF16)
    return hi, mid, lo


def _dot_exact_rhs(x, m):
    hi, mid, lo = _split3(x)
    d = lambda a: jnp.dot(a, m, preferred_element_type=F32)
    return d(hi) + d(mid) + d(lo)


def _dot_exact_lhs(m, x):
    hi, mid, lo = _split3(x)
    d = lambda a: jnp.dot(m, a, preferred_element_type=F32)
    return d(hi) + d(mid) + d(lo)


def _rmsnorm_kernel(x_ref, g_ref, o_ref):
    xf = x_ref[...]
    ms = jnp.mean(xf * xf, axis=-1, keepdims=True)
    o_ref[...] = (xf * lax.rsqrt(ms + NORM_EPS) * g_ref[...]).astype(o_ref.dtype)


def _rmsnorm(x2d, g, tm):
    m, d = x2d.shape
    tm = min(tm, m)
    return pl.pallas_call(
        _rmsnorm_kernel,
        out_shape=jax.ShapeDtypeStruct((m, d), BF16),
        grid=(m // tm,),
        in_specs=[pl.BlockSpec((tm, d), lambda i: (i, 0)),
                  pl.BlockSpec((1, d), lambda i: (0, 0))],
        out_specs=pl.BlockSpec((tm, d), lambda i: (i, 0)),
        compiler_params=_params("parallel"),
        name="rmsnorm",
    )(x2d, g.reshape(1, d))


def _mm_kernel(h_ref, w_ref, o_ref, *, act):
    acc = jnp.dot(h_ref[...], w_ref[...], preferred_element_type=F32)
    if act == "silu":
        acc = acc * _sigmoid(acc)
    elif act == "sigmoid":
        acc = _sigmoid(acc)
    o_ref[...] = acc.astype(o_ref.dtype)


def _matmul(h, w, *, act, out_dtype, tm, tn, name):
    m, k = h.shape
    n = w.shape[1]
    tm = min(tm, m)
    return pl.pallas_call(
        functools.partial(_mm_kernel, act=act),
        out_shape=jax.ShapeDtypeStruct((m, n), out_dtype),
        grid=(m // tm, n // tn),
        in_specs=[pl.BlockSpec((tm, k), lambda i, j: (i, 0)),
                  pl.BlockSpec((k, tn), lambda i, j: (0, j))],
        out_specs=pl.BlockSpec((tm, tn), lambda i, j: (i, j)),
        compiler_params=_params("parallel", "arbitrary"),
        name=name,
    )(h, w)


def _rwkv_prep_kernel(p_ref, prev_ref, mu_ref, wda_ref, dbase_ref, ibase_ref, kk_ref, ka_ref,
                      rk_ref, tri_ref, ones_ref,
                      at_ref, rt_ref, bt_ref, kt_ref, bd_ref, kd_ref, v_ref, bonus_ref, wc_ref,
                      *, blocks_per_seq):
    tp = p_ref.shape[0]
    w = RWKV_WIDTH
    x = p_ref[...]
    first = (pl.program_id(0) % blocks_per_seq) == 0
    prev_last = jnp.where(first, 0.0, prev_ref[7:8, :])
    row = lax.broadcasted_iota(jnp.int32, x.shape, 0)
    xprev = jnp.where(row == 0, prev_last, pltpu.roll(x, 1, 0))
    ps = x + mu_ref[...] * (xprev - x)
    r = ps[:, 0:w]
    k = ps[:, w:2 * w]
    v = ps[:, 2 * w:3 * w]
    z = ps[:, 3 * w:3 * w + LANES]
    lane = lax.broadcasted_iota(jnp.int32, z.shape, 1)
    z = jnp.where(lane < DECAY_RANK, jnp.tanh(z), z)
    da = jnp.dot(z, wda_ref[...], preferred_element_type=F32, precision=lax.Precision.HIGHEST)
    w_pre = dbase_ref[...] + da[:, :w]
    a = _sigmoid(ibase_ref[...] + da[:, w:])
    nz = -w_pre
    softplus = jnp.maximum(nz, 0.0) + jnp.log(1.0 + jnp.exp(-jnp.abs(nz)))
    logw = -jnp.exp(-softplus - 0.5)

    cs = _dot_exact_lhs(tri_ref[...], logw)
    cse = cs - logw
    lasts = [jnp.broadcast_to(cs[c * CHUNK + CHUNK - 1:c * CHUNK + CHUNK, :], (CHUNK, w))
             for c in range(tp // CHUNK)]
    cs_last = jnp.concatenate(lasts, axis=0)

    def head_sum(t):
        cols = [_dot_exact_rhs(t[:, g * LANES:(g + 1) * LANES], ones_ref[...])
                for g in range(w // LANES)]
        return jnp.concatenate(cols, axis=1)

    kk = k * kk_ref[...]
    nrm = jnp.maximum(jnp.sqrt(head_sum(kk * kk)), 1e-12)
    kk = kk / nrm
    kp = k * (1.0 + (a - 1.0) * ka_ref[...])
    b = kk * a
    e_neg = jnp.exp(-cs)
    e_d = jnp.exp(cs_last - cs)
    at_ref[...] = (-kk * jnp.exp(cse)).astype(BF16)
    rt_ref[...] = (r * jnp.exp(cs)).astype(BF16)
    bt_ref[...] = (b * e_neg).astype(BF16)
    kt_ref[...] = (kp * e_neg).astype(BF16)
    bd_ref[...] = (b * e_d).astype(BF16)
    kd_ref[...] = (kp * e_d).astype(BF16)
    v_ref[...] = v.astype(BF16)
    bonus_ref[...] = head_sum(r * kp * rk_ref[...]) * v
    for c in range(tp // CHUNK):
        wc_ref[0, c:c + 1, :] = jnp.exp(cs[c * CHUNK + CHUNK - 1:c * CHUNK + CHUNK, :])


def _rwkv_prep(p, seq_len, mu, wda, dbase, ibase, k_k, k_a, r_k, tp):
    n = p.shape[0]
    w = RWKV_WIDTH
    tp = min(tp, seq_len)
    nblk = n // tp
    cpb = tp // CHUNK
    tri = np.tril(np.ones((CHUNK, CHUNK), np.float32))
    tri = jnp.asarray(np.kron(np.eye(cpb, dtype=np.float32), tri), BF16)
    ones_bd = jnp.asarray(np.kron(np.eye(LANES // RWKV_HEAD, dtype=np.float32),
                                  np.ones((RWKV_HEAD, RWKV_HEAD), np.float32)), BF16)
    row = lambda a: a.reshape(1, -1)
    full = lambda shape: pl.BlockSpec(shape, lambda i: (0,) * len(shape))
    tok = pl.BlockSpec((tp, w), lambda i: (i, 0))
    outs = pl.pallas_call(
        functools.partial(_rwkv_prep_kernel, blocks_per_seq=seq_len // tp),
        out_shape=[jax.ShapeDtypeStruct((n, w), BF16)] * 7
        + [jax.ShapeDtypeStruct((n, w), F32), jax.ShapeDtypeStruct((nblk, cpb, w), F32)],
        grid=(nblk,),
        in_specs=[pl.BlockSpec((tp, SHIFT_COLS), lambda i: (i, 0)),
                  pl.BlockSpec((8, SHIFT_COLS), lambda i: (jnp.maximum(i * (tp // 8) - 1, 0), 0)),
                  full((1, SHIFT_COLS)), full((LANES, 2 * w)), full((1, w)), full((1, w)),
                  full((1, w)), full((1, w)), full((1, w)), full((tp, tp)), full((LANES, LANES))],
        out_specs=[tok] * 8 + [pl.BlockSpec((1, cpb, w), lambda i: (i, 0, 0))],
        compiler_params=_params("parallel"),
        name="rwkv_prep",
    )(p, p, row(mu), wda, row(dbase), row(ibase), row(k_k), row(k_a), row(r_k), tri, ones_bd)
    return outs


def _rwkv_chunk_kernel(at_ref, rt_ref, bt_ref, kt_ref, bd_ref, kd_ref, v_ref, bonus_ref, wc_ref,
                       gate_ref, gnw_ref, gnb_ref, avg_ref, o_ref,
                       pt_ref, qt_ref, g_ref, y0_ref, *, n_chunks):
    c2 = 2 * CHUNK
    lane = lax.broadcasted_iota(jnp.int32, (CHUNK, PAIR), 1)
    head0 = lane < RWKV_HEAD
    ri = lax.broadcasted_iota(jnp.int32, (2 * c2, 2 * c2), 0)
    ci = lax.broadcasted_iota(jnp.int32, (2 * c2, 2 * c2), 1)
    rt_ = ri & (CHUNK - 1)
    ct_ = ci & (CHUNK - 1)
    keep = ct_ < rt_ + jnp.where(ri < c2, 0, 1)
    eye = (lax.broadcasted_iota(jnp.int32, (c2, c2), 0)
           == lax.broadcasted_iota(jnp.int32, (c2, c2), 1)).astype(F32)

    def stacked(ref, sl):
        x = ref[sl, :]
        zero = jnp.zeros_like(x)
        return jnp.concatenate([jnp.where(head0, x, zero), jnp.where(head0, zero, x)], axis=0)

    def mm(a, b):
        return jnp.dot(a.astype(BF16), b.astype(BF16), preferred_element_type=F32)

    def phase1(c, carry):
        sl = pl.ds(pl.multiple_of(c * CHUNK, CHUNK), CHUNK)
        d_a, d_r = stacked(at_ref, sl), stacked(rt_ref, sl)
        d_b, d_k = stacked(bt_ref, sl), stacked(kt_ref, sl)
        d_bd, d_kd, d_v = stacked(bd_ref, sl), stacked(kd_ref, sl), stacked(v_ref, sl)
        lm = lax.dot_general(jnp.concatenate([d_a, d_r], axis=0),
                             jnp.concatenate([d_b, d_k], axis=0), _NT,
                             preferred_element_type=F32)
        lm = jnp.where(keep, lm, 0.0)
        l_ab, l_ak = lm[:c2, :c2], lm[:c2, c2:]
        m_rb, m_rk = lm[c2:, :c2], lm[c2:, c2:]
        inv = eye + l_ab
        lp = l_ab
        for _ in range(int(np.log2(CHUNK)) - 1):
            lp = mm(lp, lp)
            inv = inv + mm(lp, inv)
        lak_v = mm(l_ak, d_v)
        xu = mm(inv, jnp.concatenate([d_a, lak_v.astype(BF16)], axis=1))
        xu_b = xu.astype(BF16)
        gy = mm(m_rb, xu_b)
        g_ref[c] = (d_r.astype(F32) + gy[:, :c2]).astype(BF16)
        y0_ref[c] = gy[:, c2:] + mm(m_rk, d_v)
        pq = lax.dot_general(xu_b, d_bd, _TN, preferred_element_type=F32)
        pt_ref[c] = pq[:c2, :].astype(BF16)
        qt_ref[c] = pq[c2:, :] + lax.dot_general(d_v, d_kd, _TN, preferred_element_type=F32)
        return carry

    lax.fori_loop(0, n_chunks, phase1, 0)

    gnw = gnw_ref[...]
    gnb = gnb_ref[...]
    avg = avg_ref[...]

    def phase2(c, st):
        sl = pl.ds(pl.multiple_of(c * CHUNK, CHUNK), CHUNK)
        st_b = st.astype(BF16)
        y_d = lax.dot_general(g_ref[c], st_b, _NT, preferred_element_type=F32) + y0_ref[c]
        st_new = (st * wc_ref[0, pl.ds(c, 1), :]
                  + jnp.dot(st_b, pt_ref[c], preferred_element_type=F32) + qt_ref[c])
        y = y_d[:CHUNK, :] + y_d[CHUNK:, :]
        mean = _dot_exact_rhs(y, avg)
        d = y - mean
        var = _dot_exact_rhs(d * d, avg)
        yn = d * lax.rsqrt(var + GN_EPS) * gnw + gnb
        o_ref[sl, :] = ((yn + bonus_ref[sl, :]) * gate_ref[sl, :].astype(F32)).astype(o_ref.dtype)
        return st_new

    lax.fori_loop(0, n_chunks, phase2, jnp.zeros((c2, c2), F32))


def _rwkv_chunk(prep, gates, gate_col, gn_w, gn_b, batch, seq_len):
    at, rt, bt, kt, bd, kd, v, bonus, wc = prep
    n, w = at.shape
    n_chunks = seq_len // CHUNK
    n_pairs = w // PAIR
    wc = wc.reshape(batch, n_chunks, w)
    avg = jnp.asarray(np.kron(np.eye(PAIR // RWKV_HEAD, dtype=np.float32),
                              np.full((RWKV_HEAD, RWKV_HEAD), 1.0 / RWKV_HEAD, np.float32)), BF16)
    tok = pl.BlockSpec((seq_len, PAIR), lambda b, h: (b, h))
    gate_spec = pl.BlockSpec((seq_len, PAIR), lambda b, h: (b, gate_col // PAIR + h))
    vec = pl.BlockSpec((1, PAIR), lambda b, h: (0, h))
    c2 = 2 * CHUNK
    return pl.pallas_call(
        functools.partial(_rwkv_chunk_kernel, n_chunks=n_chunks),
        out_shape=jax.ShapeDtypeStruct((n, w), BF16),
        grid=(batch, n_pairs),
        in_specs=[tok] * 8 + [pl.BlockSpec((1, n_chunks, PAIR), lambda b, h: (b, 0, h)),
                              gate_spec, vec, vec,
                              pl.BlockSpec((PAIR, PAIR), lambda b, h: (0, 0))],
        out_specs=tok,
        scratch_shapes=[pltpu.VMEM((n_chunks, c2, c2), BF16), pltpu.VMEM((n_chunks, c2, c2), F32),
                        pltpu.VMEM((n_chunks, c2, c2), BF16), pltpu.VMEM((n_chunks, c2, c2), F32)],
        compiler_params=_params("parallel", "parallel"),
        name="rwkv_chunk",
    )(at, rt, bt, kt, bd, kd, v, bonus, wc, gates, gn_w.reshape(1, w), gn_b.reshape(1, w), avg)


def _swa_kernel(sink_ref, q_ref, kp_ref, kc_ref, vp_ref, vc_ref, g_ref, o_ref):
    n = pl.program_id(1)
    lane = lax.broadcasted_iota(jnp.int32, (2 * BLOCK, LANES), 1)
    head0 = lane < SWA_HEAD
    r = lax.broadcasted_iota(jnp.int32, (BLOCK, 2 * BLOCK), 0)
    c = lax.broadcasted_iota(jnp.int32, (BLOCK, 2 * BLOCK), 1)
    valid = (c > r) & (c <= r + BLOCK) & ((c >= BLOCK) | (n > 0))
    pairs_per_kv = SWA_Q_HEADS // SWA_KV_HEADS // 2

    def block_diag(prev, cur):
        x = jnp.concatenate([prev, cur], axis=0)
        zero = jnp.zeros_like(x)
        return jnp.concatenate([jnp.where(head0, x, zero), jnp.where(head0, zero, x)], axis=0)

    for h in range(SWA_KV_HEADS):
        hs = slice(h * LANES, (h + 1) * LANES)
        kblk = block_diag(kp_ref[:, hs], kc_ref[:, hs])
        vblk = block_diag(vp_ref[:, hs], vc_ref[:, hs])
        for j in range(pairs_per_kv):
            pair = h * pairs_per_kv + j
            ps_ = slice(pair * LANES, (pair + 1) * LANES)
            s = lax.dot_general(q_ref[:, ps_], kblk, _NT, preferred_element_type=F32)
            probs = []
            for e in range(2):
                se = jnp.where(valid, s[:, e * 2 * BLOCK:(e + 1) * 2 * BLOCK], -jnp.inf)
                sink = sink_ref[2 * pair + e]
                m = jnp.maximum(jnp.max(se, axis=-1, keepdims=True), sink)
                ex = jnp.exp(se - m)
                den = jnp.sum(ex, axis=-1, keepdims=True) + jnp.exp(sink - m)
                probs.append((ex / den).astype(BF16))
            o = jnp.dot(jnp.concatenate(probs, axis=1), vblk, preferred_element_type=F32)
            o_ref[:, ps_] = (o * g_ref[:, ps_].astype(F32)).astype(o_ref.dtype)


def _swa(plain, q_col, k_col, v_col, gates, gate_col, sinks, batch, seq_len):
    n = plain.shape[0]
    w = SWA_WIDTH
    nb = seq_len // BLOCK
    kvw = 2 * SWA_KV_HEADS * SWA_HEAD
    cur = lambda col, width: (lambda b, i: (b * nb + i, col // width))
    prev = lambda col, width: (lambda b, i: (b * nb + jnp.maximum(i - 1, 0), col // width))
    return pl.pallas_call(
        _swa_kernel,
        out_shape=jax.ShapeDtypeStruct((n, w), BF16),
        grid=(batch, nb),
        in_specs=[pl.BlockSpec(memory_space=pltpu.SMEM),
                  pl.BlockSpec((BLOCK, w), cur(q_col, w)),
                  pl.BlockSpec((BLOCK, kvw), prev(k_col, kvw)), pl.BlockSpec((BLOCK, kvw), cur(k_col, kvw)),
                  pl.BlockSpec((BLOCK, kvw), prev(v_col, kvw)), pl.BlockSpec((BLOCK, kvw), cur(v_col, kvw)),
                  pl.BlockSpec((BLOCK, w), cur(gate_col, w))],
        out_specs=pl.BlockSpec((BLOCK, w), cur(0, w)),
        compiler_params=_params("parallel", "parallel"),
        name="swa",
    )(sinks, plain, plain, plain, plain, plain, gates)


def _xattn_kernel(q_ref, kv_ref, g_ref, o_ref):
    for h in range(XA_HEADS):
        hs = slice(h * XA_HEAD, (h + 1) * XA_HEAD)
        vs = slice(XA_WIDTH + h * XA_HEAD, XA_WIDTH + (h + 1) * XA_HEAD)
        s = lax.dot_general(q_ref[:, hs], kv_ref[:, hs], _NT, preferred_element_type=F32)
        m = jnp.max(s, axis=-1, keepdims=True)
        ex = jnp.exp(s - m)
        p = (ex / jnp.sum(ex, axis=-1, keepdims=True)).astype(BF16)
        o = jnp.dot(p, kv_ref[:, vs], preferred_element_type=F32)
        o_ref[:, hs] = (o * g_ref[:, hs].astype(F32)).astype(o_ref.dtype)


def _xattn(plain, q_col, kv, gates, gate_col, batch, seq_len, tq):
    n = plain.shape[0]
    w = XA_WIDTH
    tq = min(tq, seq_len)
    nq = seq_len // tq
    m = kv.shape[0] // batch
    tok = lambda col: pl.BlockSpec((tq, w), lambda b, i: (b * nq + i, col // w))
    return pl.pallas_call(
        _xattn_kernel,
        out_shape=jax.ShapeDtypeStruct((n, w), BF16),
        grid=(batch, nq),
        in_specs=[tok(q_col), pl.BlockSpec((m, kv.shape[1]), lambda b, i: (b, 0)), tok(gate_col)],
        out_specs=tok(0),
        compiler_params=_params("parallel", "parallel"),
        name="xattn",
    )(plain, kv, gates)


def _merge_kernel(ya_ref, yb_ref, yc_ref, ga_ref, gb_ref, gc_ref, wa_ref, wb_ref, wc_ref, o_ref):
    d = lambda y, w: jnp.dot(y[...], w[...], preferred_element_type=F32)
    acc = ga_ref[...].astype(F32) * d(ya_ref, wa_ref)
    acc += gb_ref[...].astype(F32) * d(yb_ref, wb_ref)
    acc += gc_ref[...].astype(F32) * d(yc_ref, wc_ref)
    o_ref[...] = acc.astype(o_ref.dtype)


def _merge(ya, yb, yc, gates, wa, wb, wc, tm, tn):
    n = ya.shape[0]
    d = wa.shape[1]
    tm = min(tm, n)
    nj = d // tn
    y_spec = lambda a: pl.BlockSpec((tm, a.shape[1]), lambda i, j: (i, 0))
    w_spec = lambda a: pl.BlockSpec((a.shape[0], tn), lambda i, j: (0, j))
    g_spec = lambda br: pl.BlockSpec((tm, tn), lambda i, j: (i, br * nj + j))
    return pl.pallas_call(
        _merge_kernel,
        out_shape=jax.ShapeDtypeStruct((n, d), BF16),
        grid=(n // tm, nj),
        in_specs=[y_spec(ya), y_spec(yb), y_spec(yc), g_spec(0), g_spec(1), g_spec(2),
                  w_spec(wa), w_spec(wb), w_spec(wc)],
        out_specs=pl.BlockSpec((tm, tn), lambda i, j: (i, j)),
        compiler_params=_params("parallel", "arbitrary"),
        name="merge",
    )(ya, yb, yc, gates, gates, gates, wa, wb, wc)


def _out_kernel(m_ref, w_ref, x_ref, g_ref, o_ref):
    o = jnp.dot(m_ref[...], w_ref[...], preferred_element_type=F32)
    ms = jnp.mean(o * o, axis=-1, keepdims=True)
    o_ref[...] = x_ref[...] + o * lax.rsqrt(ms + NORM_EPS) * g_ref[...]


def _out_proj(merged, w_out, x2d, g_post, tm):
    n, d = x2d.shape
    tm = min(tm, n)
    tok = pl.BlockSpec((tm, d), lambda i: (i, 0))
    return pl.pallas_call(
        _out_kernel,
        out_shape=jax.ShapeDtypeStruct((n, d), F32),
        grid=(n // tm,),
        in_specs=[tok, pl.BlockSpec((d, d), lambda i: (0, 0)), tok,
                  pl.BlockSpec((1, d), lambda i: (0, 0))],
        out_specs=tok,
        compiler_params=_params("parallel"),
        name="out_proj",
    )(merged, w_out, x2d, g_post.reshape(1, d))


def _dup_heads(wcols):
    d, w = wcols.shape
    h = w // SWA_HEAD
    return jnp.broadcast_to(wcols.reshape(d, h, 1, SWA_HEAD), (d, h, 2, SWA_HEAD)).reshape(d, 2 * w)


def kernel(x, mem, g_pre, w_in, mu_shift, decay_base, decay_up, iclr_base, iclr_up, k_k, k_a, r_k,
           gn_w, gn_b, attn_sinks, g_mem, w_mem_kv, w_up_rwkv, w_up_swa, w_up_xattn, w_out, g_post):
    batch, seq_len, d = x.shape
    n = batch * seq_len
    x2d = x.reshape(n, d)
    mem2d = mem.reshape(batch * mem.shape[1], d)
    w = RWKV_WIDTH
    kvh = SWA_KV_HEADS * SWA_HEAD
    o_gate_a = SHIFT_COLS
    o_q_b = o_gate_a + w
    o_kv_b = o_q_b + SWA_WIDTH
    o_gate_b = o_kv_b + 2 * kvh
    o_q_c = o_gate_b + SWA_WIDTH
    o_gate_c = o_q_c + XA_WIDTH
    o_merge = o_gate_c + XA_WIDTH

    for l in range(DEPTH):
        wl = w_in[l]
        w_shift = wl[:, :SHIFT_COLS].astype(BF16)
        w_gates = jnp.concatenate([wl[:, o_gate_a:o_q_b], wl[:, o_gate_b:o_q_c],
                                   wl[:, o_gate_c:o_merge]], axis=1).astype(BF16)
        w_plain = jnp.concatenate(
            [wl[:, o_q_b:o_kv_b] * (SWA_HEAD ** -0.5), wl[:, o_q_c:o_gate_c] * (XA_HEAD ** -0.5),
             _dup_heads(wl[:, o_kv_b:o_gate_b])], axis=1).astype(BF16)
        w_merge = wl[:, o_merge:].astype(BF16)
        wda = jnp.zeros((LANES, 2 * w), F32)
        wda = wda.at[:DECAY_RANK, :w].set(decay_up[l]).at[DECAY_RANK:, w:].set(iclr_up[l])

        h = _rmsnorm(x2d, g_pre[l], 512)
        p_shift = _matmul(h, w_shift, act=None, out_dtype=F32, tm=1024, tn=640, name="proj_shift")
        p_gates = _matmul(h, w_gates, act="silu", out_dtype=BF16, tm=1024, tn=1024, name="proj_gates")
        p_plain = _matmul(h, w_plain, act=None, out_dtype=BF16, tm=1024, tn=512, name="proj_plain")
        p_merge = _matmul(h, w_merge, act="sigmoid", out_dtype=BF16, tm=1024, tn=1024, name="proj_merge")

        k_col = SWA_WIDTH + XA_WIDTH
        prep = _rwkv_prep(p_shift, seq_len, mu_shift[l], wda, decay_base[l], iclr_base[l],
                          k_k[l], k_a[l], r_k[l].reshape(-1), 256)
        y_a = _rwkv_chunk(prep, p_gates, 0, gn_w[l], gn_b[l], batch, seq_len)
        y_b = _swa(p_plain, 0, k_col, k_col + 2 * kvh, p_gates, w, attn_sinks[l], batch, seq_len)

        mem_n = _rmsnorm(mem2d, g_mem[l], 512)
        kv_c = _matmul(mem_n, w_mem_kv[l].astype(BF16), act=None, out_dtype=BF16, tm=1024, tn=1024,
                       name="mem_kv")
        y_c = _xattn(p_plain, SWA_WIDTH, kv_c, p_gates, w + SWA_WIDTH, batch, seq_len, 512)

        merged = _merge(y_a, y_b, y_c, p_merge, w_up_rwkv[l].astype(BF16), w_up_swa[l].astype(BF16),
                        w_up_xattn[l].astype(BF16), 1024, 512)
        x2d = _out_proj(merged, w_out[l].astype(BF16), x2d, g_post[l], 512)
    return x2d.reshape(batch, seq_len, d)
```

```python
import functools

import jax
import jax.numpy as jnp
import numpy as np
from jax import lax
from jax.experimental import pallas as pl
from jax.experimental.pallas import tpu as pltpu

F32 = jnp.float32
BF16 = jnp.bfloat16

D_MODEL = 2048
DEPTH = 2
MEM_LEN = 256
NORM_EPS = 1e-6
RWKV_WIDTH = 1024
RWKV_HEAD = 64
DECAY_RANK = 64
ICLR_RANK = 64
GN_EPS = 64e-5
SWA_HEAD = 64
SWA_Q_HEADS = 16
SWA_KV_HEADS = 2
SWA_WIDTH = SWA_Q_HEADS * SWA_HEAD
BLOCK = 128
XA_HEADS = 4
XA_HEAD = 256
XA_WIDTH = XA_HEADS * XA_HEAD
N_BRANCH = 3
SHIFT_COLS = 3 * RWKV_WIDTH + DECAY_RANK + ICLR_RANK

LANES = 128
CHUNK = 64
PAIR = 2 * RWKV_HEAD
VMEM_LIMIT = 56 * 1024 * 1024

_NT = (((1,), (1,)), ((), ()))
_TN = (((0,), (0,)), ((), ()))


def _params(*sem):
    return pltpu.CompilerParams(dimension_semantics=sem, vmem_limit_bytes=VMEM_LIMIT)


def _sigmoid(x):
    return 1.0 / (1.0 + jnp.exp(-x))


def _split3(x):
    hi = x.astype(BF16)
    r1 = x - hi.astype(F32)
    mid = r1.astype(BF16)
    lo = (r1 - mid.astype(F32)).astype(BF16)
    return hi, mid, lo


def _dot_exact_rhs(x, m):
    hi, mid, lo = _split3(x)
    d = lambda a: jnp.dot(a, m, preferred_element_type=F32)
    return d(hi) + d(mid) + d(lo)


def _dot_exact_lhs(m, x):
    hi, mid, lo = _split3(x)
    d = lambda a: jnp.dot(m, a, preferred_element_type=F32)
    return d(hi) + d(mid) + d(lo)


def _rmsnorm_kernel(x_ref, g_ref, o_ref):
    xf = x_ref[...]
    ms = jnp.mean(xf * xf, axis=-1, keepdims=True)
    o_ref[...] = (xf * lax.rsqrt(ms + NORM_EPS) * g_ref[...]).astype(o_ref.dtype)


def _rmsnorm(x2d, g, tm):
    m, d = x2d.shape
    tm = min(tm, m)
    return pl.pallas_call(
        _rmsnorm_kernel,
        out_shape=jax.ShapeDtypeStruct((m, d), BF16),
        grid=(m // tm,),
        in_specs=[pl.BlockSpec((tm, d), lambda i: (i, 0)),
                  pl.BlockSpec((1, d), lambda i: (0, 0))],
        out_specs=pl.BlockSpec((tm, d), lambda i: (i, 0)),
        compiler_params=_params("parallel"),
        name="rmsnorm",
    )(x2d, g.reshape(1, d))


def _mm_kernel(h_ref, w_ref, o_ref, *, act):
    acc = jnp.dot(h_ref[...], w_ref[...], preferred_element_type=F32)
    if act == "silu":
        acc = acc * _sigmoid(acc)
    elif act == "sigmoid":
        acc = _sigmoid(acc)
    o_ref[...] = acc.astype(o_ref.dtype)


def _matmul(h, w, *, act, out_dtype, tm, tn, name):
    m, k = h.shape
    n = w.shape[1]
    tm = min(tm, m)
    return pl.pallas_call(
        functools.partial(_mm_kernel, act=act),
        out_shape=jax.ShapeDtypeStruct((m, n), out_dtype),
        grid=(m // tm, n // tn),
        in_specs=[pl.BlockSpec((tm, k), lambda i, j: (i, 0)),
                  pl.BlockSpec((k, tn), lambda i, j: (0, j))],
        out_specs=pl.BlockSpec((tm, tn), lambda i, j: (i, j)),
        compiler_params=_params("parallel", "arbitrary"),
        name=name,
    )(h, w)


def _rwkv_prep_kernel(p_ref, prev_ref, mu_ref, wda_ref, dbase_ref, ibase_ref, kk_ref, ka_ref,
                      rk_ref, tri_ref, ones_ref,
                      at_ref, rt_ref, bt_ref, kt_ref, bd_ref, kd_ref, v_ref, bonus_ref, wc_ref,
                      *, blocks_per_seq):
    tp = p_ref.shape[0]
    w = RWKV_WIDTH
    x = p_ref[...]
    first = (pl.program_id(0) % blocks_per_seq) == 0
    prev_last = jnp.where(first, 0.0, prev_ref[7:8, :])
    row = lax.broadcasted_iota(jnp.int32, x.shape, 0)
    xprev = jnp.where(row == 0, prev_last, pltpu.roll(x, 1, 0))
    ps = x + mu_ref[...] * (xprev - x)
    r = ps[:, 0:w]
    k = ps[:, w:2 * w]
    v = ps[:, 2 * w:3 * w]
    z = ps[:, 3 * w:3 * w + LANES]
    lane = lax.broadcasted_iota(jnp.int32, z.shape, 1)
    z = jnp.where(lane < DECAY_RANK, jnp.tanh(z), z)
    da = jnp.dot(z, wda_ref[...], preferred_element_type=F32, precision=lax.Precision.HIGHEST)
    w_pre = dbase_ref[...] + da[:, :w]
    a = _sigmoid(ibase_ref[...] + da[:, w:])
    nz = -w_pre
    softplus = jnp.maximum(nz, 0.0) + jnp.log(1.0 + jnp.exp(-jnp.abs(nz)))
    logw = -jnp.exp(-softplus - 0.5)

    cs = _dot_exact_lhs(tri_ref[...], logw)
    cse = cs - logw
    lasts = [jnp.broadcast_to(cs[c * CHUNK + CHUNK - 1:c * CHUNK + CHUNK, :], (CHUNK, w))
             for c in range(tp // CHUNK)]
    cs_last = jnp.concatenate(lasts, axis=0)

    def head_sum(t):
        cols = [_dot_exact_rhs(t[:, g * LANES:(g + 1) * LANES], ones_ref[...])
                for g in range(w // LANES)]
        return jnp.concatenate(cols, axis=1)

    kk = k * kk_ref[...]
    nrm = jnp.maximum(jnp.sqrt(head_sum(kk * kk)), 1e-12)
    kk = kk / nrm
    kp = k * (1.0 + (a - 1.0) * ka_ref[...])
    b = kk * a
    e_neg = jnp.exp(-cs)
    e_d = jnp.exp(cs_last - cs)
    at_ref[...] = (-kk * jnp.exp(cse)).astype(BF16)
    rt_ref[...] = (r * jnp.exp(cs)).astype(BF16)
    bt_ref[...] = (b * e_neg).astype(BF16)
    kt_ref[...] = (kp * e_neg).astype(BF16)
    bd_ref[...] = (b * e_d).astype(BF16)
    kd_ref[...] = (kp * e_d).astype(BF16)
    v_ref[...] = v.astype(BF16)
    bonus_ref[...] = head_sum(r * kp * rk_ref[...]) * v
    for c in range(tp // CHUNK):
        wc_ref[0, c:c + 1, :] = jnp.exp(cs[c * CHUNK + CHUNK - 1:c * CHUNK + CHUNK, :])


def _rwkv_prep(p, seq_len, mu, wda, dbase, ibase, k_k, k_a, r_k, tp):
    n = p.shape[0]
    w = RWKV_WIDTH
    tp = min(tp, seq_len)
    nblk = n // tp
    cpb = tp // CHUNK
    tri = np.tril(np.ones((CHUNK, CHUNK), np.float32))
    tri = jnp.asarray(np.kron(np.eye(cpb, dtype=np.float32), tri), BF16)
    ones_bd = jnp.asarray(np.kron(np.eye(LANES // RWKV_HEAD, dtype=np.float32),
                                  np.ones((RWKV_HEAD, RWKV_HEAD), np.float32)), BF16)
    row = lambda a: a.reshape(1, -1)
    full = lambda shape: pl.BlockSpec(shape, lambda i: (0,) * len(shape))
    tok = pl.BlockSpec((tp, w), lambda i: (i, 0))
    outs = pl.pallas_call(
        functools.partial(_rwkv_prep_kernel, blocks_per_seq=seq_len // tp),
        out_shape=[jax.ShapeDtypeStruct((n, w), BF16)] * 7
        + [jax.ShapeDtypeStruct((n, w), F32), jax.ShapeDtypeStruct((nblk, cpb, w), F32)],
        grid=(nblk,),
        in_specs=[pl.BlockSpec((tp, SHIFT_COLS), lambda i: (i, 0)),
                  pl.BlockSpec((8, SHIFT_COLS), lambda i: (jnp.maximum(i * (tp // 8) - 1, 0), 0)),
                  full((1, SHIFT_COLS)), full((LANES, 2 * w)), full((1, w)), full((1, w)),
                  full((1, w)), full((1, w)), full((1, w)), full((tp, tp)), full((LANES, LANES))],
        out_specs=[tok] * 8 + [pl.BlockSpec((1, cpb, w), lambda i: (i, 0, 0))],
        compiler_params=_params("parallel"),
        name="rwkv_prep",
    )(p, p, row(mu), wda, row(dbase), row(ibase), row(k_k), row(k_a), row(r_k), tri, ones_bd)
    return outs


def _rwkv_chunk_kernel(at_ref, rt_ref, bt_ref, kt_ref, bd_ref, kd_ref, v_ref, bonus_ref, wc_ref,
                       gate_ref, gnw_ref, gnb_ref, avg_ref, o_ref,
                       pt_ref, qt_ref, g_ref, y0_ref, y_ref, *, n_chunks, n_pairs):
    c2 = 2 * CHUNK
    lane = lax.broadcasted_iota(jnp.int32, (CHUNK, PAIR), 1)
    head0 = lane < RWKV_HEAD
    ri = lax.broadcasted_iota(jnp.int32, (2 * c2, 2 * c2), 0)
    ci = lax.broadcasted_iota(jnp.int32, (2 * c2, 2 * c2), 1)
    keep = (ci & (CHUNK - 1)) < (ri & (CHUNK - 1)) + jnp.where(ri < c2, 0, 1)
    eye = (lax.broadcasted_iota(jnp.int32, (c2, c2), 0)
           == lax.broadcasted_iota(jnp.int32, (c2, c2), 1)).astype(F32)
    zeros_b = jnp.zeros((c2, c2), BF16)

    def stacked(ref, sl, ls):
        x = ref[sl, ls]
        zero = jnp.zeros_like(x)
        return jnp.concatenate([jnp.where(head0, x, zero), jnp.where(head0, zero, x)], axis=0)

    def dot(a, b):
        return jnp.dot(a, b, preferred_element_type=F32)

    def phase1_group(items):
        def ld(ref, it):
            p, c = it
            return stacked(ref, pl.ds(pl.multiple_of(c * CHUNK, CHUNK), CHUNK),
                           slice(p * PAIR, (p + 1) * PAIR))

        lms = [lax.dot_general(jnp.concatenate([ld(at_ref, it), ld(rt_ref, it)], axis=0),
                               jnp.concatenate([ld(bt_ref, it), ld(kt_ref, it)], axis=0), _NT,
                               preferred_element_type=F32) for it in items]
        lms = [jnp.where(keep, lm, 0.0) for lm in lms]
        lak_bs = [lm[:c2, c2:].astype(BF16) for lm in lms]
        m_rs = [lm[c2:, :].astype(BF16) for lm in lms]
        invs = [eye + lm[:c2, :c2] for lm in lms]
        l_bs = [lm[:c2, :c2].astype(BF16) for lm in lms]
        lps = [dot(l_b, l_b) for l_b in l_bs]
        lak_vs = [dot(lak_b, ld(v_ref, it)).astype(BF16) for lak_b, it in zip(lak_bs, items)]
        n_fac = int(np.log2(CHUNK)) - 1
        for f in range(n_fac - 1):
            lp_bs = [lp.astype(BF16) for lp in lps]
            prs = [dot(lp_b, jnp.concatenate([lp_b, inv.astype(BF16)], axis=1))
                   for lp_b, inv in zip(lp_bs, invs)]
            lps = [pr[:, :c2] for pr in prs]
            invs = [inv + pr[:, c2:] for inv, pr in zip(invs, prs)]
        invs = [inv + dot(lp.astype(BF16), inv.astype(BF16)) for lp, inv in zip(lps, invs)]
        xu_bs = [dot(inv.astype(BF16), jnp.concatenate([ld(at_ref, it), lak_v], axis=1)).astype(BF16)
                 for inv, lak_v, it in zip(invs, lak_vs, items)]
        rhss = [jnp.concatenate([xu_b, jnp.concatenate([zeros_b, ld(v_ref, it)], axis=1)], axis=0)
                for xu_b, it in zip(xu_bs, items)]
        gys = [dot(m_r, rhs) for m_r, rhs in zip(m_rs, rhss)]
        pqs = [lax.dot_general(rhs, jnp.concatenate([ld(bd_ref, it), ld(kd_ref, it)], axis=0), _TN,
                               preferred_element_type=F32)
               for rhs, it in zip(rhss, items)]
        for (p, c), gy, pq in zip(items, gys, pqs):
            g_ref[p, c] = (ld(rt_ref, (p, c)).astype(F32) + gy[:, :c2]).astype(BF16)
            y0_ref[p, c] = gy[:, c2:]
            pt_ref[p, c] = pq[:c2, :].astype(BF16)
            qt_ref[p, c] = pq[c2:, :]

    unroll1 = 4

    def phase1(i, carry):
        phase1_group([(p, i * unroll1 + u) for p in range(n_pairs) for u in range(unroll1)])
        return carry

    lax.fori_loop(0, n_chunks // unroll1, phase1, 0)

    def phase2_chunk(c, sts):
        sl = pl.ds(pl.multiple_of(c * CHUNK, CHUNK), CHUNK)
        new = []
        for p in range(n_pairs):
            ls = slice(p * PAIR, (p + 1) * PAIR)
            st = sts[p]
            st_b = st.astype(BF16)
            y_d = lax.dot_general(g_ref[p, c], st_b, _NT, preferred_element_type=F32) + y0_ref[p, c]
            y_ref[sl, ls] = y_d[:CHUNK, :] + y_d[CHUNK:, :]
            new.append(st * wc_ref[0, c, :, ls] + dot(st_b, pt_ref[p, c]) + qt_ref[p, c])
        return tuple(new)

    unroll2 = 2

    def phase2(i, sts):
        for u in range(unroll2):
            sts = phase2_chunk(i * unroll2 + u, sts)
        return sts

    lax.fori_loop(0, n_chunks // unroll2, phase2,
                  tuple(jnp.zeros((c2, c2), F32) for _ in range(n_pairs)))

    rows3 = 4 * CHUNK
    avg = avg_ref[...]

    def phase3(i, carry):
        sl = pl.ds(pl.multiple_of(i * rows3, rows3), rows3)
        for p in range(n_pairs):
            ls = slice(p * PAIR, (p + 1) * PAIR)
            y = y_ref[sl, ls]
            d = y - _dot_exact_rhs(y, avg)
            var = _dot_exact_rhs(d * d, avg)
            yn = d * lax.rsqrt(var + GN_EPS) * gnw_ref[:, ls] + gnb_ref[:, ls]
            o_ref[sl, ls] = ((yn + bonus_ref[sl, ls]) * gate_ref[sl, ls].astype(F32)).astype(o_ref.dtype)
        return carry

    lax.fori_loop(0, n_chunks * CHUNK // rows3, phase3, 0)


def _rwkv_chunk(prep, gates, gate_col, gn_w, gn_b, batch, seq_len):
    at, rt, bt, kt, bd, kd, v, bonus, wc = prep
    n, w = at.shape
    n_chunks = seq_len // CHUNK
    n_pairs = 2
    bw = n_pairs * PAIR
    wc = wc.reshape(batch, n_chunks, 1, w)
    avg = jnp.asarray(np.kron(np.eye(PAIR // RWKV_HEAD, dtype=np.float32),
                              np.full((RWKV_HEAD, RWKV_HEAD), 1.0 / RWKV_HEAD, np.float32)), BF16)
    tok = pl.BlockSpec((seq_len, bw), lambda b, h: (b, h))
    gate_spec = pl.BlockSpec((seq_len, bw), lambda b, h: (b, gate_col // bw + h))
    vec = pl.BlockSpec((1, bw), lambda b, h: (0, h))
    c2 = 2 * CHUNK
    ops = lambda dt: pltpu.VMEM((n_pairs, n_chunks, c2, c2), dt)
    return pl.pallas_call(
        functools.partial(_rwkv_chunk_kernel, n_chunks=n_chunks, n_pairs=n_pairs),
        out_shape=jax.ShapeDtypeStruct((n, w), BF16),
        grid=(batch, w // bw),
        in_specs=[tok] * 8 + [pl.BlockSpec((1, n_chunks, 1, bw), lambda b, h: (b, 0, 0, h)),
                              gate_spec, vec, vec,
                              pl.BlockSpec((PAIR, PAIR), lambda b, h: (0, 0))],
        out_specs=tok,
        scratch_shapes=[ops(BF16), ops(F32), ops(BF16), ops(F32), pltpu.VMEM((seq_len, bw), F32)],
        compiler_params=_params("parallel", "parallel"),
        name="rwkv_chunk",
    )(at, rt, bt, kt, bd, kd, v, bonus, wc, gates, gn_w.reshape(1, w), gn_b.reshape(1, w), avg)


def _swa_kernel(sink_ref, q_ref, kp_ref, kc_ref, vp_ref, vc_ref, g_ref, o_ref):
    n = pl.program_id(1)
    lane = lax.broadcasted_iota(jnp.int32, (2 * BLOCK, LANES), 1)
    head0 = lane < SWA_HEAD
    r = lax.broadcasted_iota(jnp.int32, (BLOCK, 2 * BLOCK), 0)
    c = lax.broadcasted_iota(jnp.int32, (BLOCK, 2 * BLOCK), 1)
    valid = (c > r) & (c <= r + BLOCK) & ((c >= BLOCK) | (n > 0))
    pairs_per_kv = SWA_Q_HEADS // SWA_KV_HEADS // 2

    def block_diag(prev, cur):
        x = jnp.concatenate([prev, cur], axis=0)
        zero = jnp.zeros_like(x)
        return jnp.concatenate([jnp.where(head0, x, zero), jnp.where(head0, zero, x)], axis=0)

    for h in range(SWA_KV_HEADS):
        hs = slice(h * LANES, (h + 1) * LANES)
        kblk = block_diag(kp_ref[:, hs], kc_ref[:, hs])
        vblk = block_diag(vp_ref[:, hs], vc_ref[:, hs])
        for j in range(pairs_per_kv):
            pair = h * pairs_per_kv + j
            ps_ = slice(pair * LANES, (pair + 1) * LANES)
            s = lax.dot_general(q_ref[:, ps_], kblk, _NT, preferred_element_type=F32)
            probs = []
            for e in range(2):
                se = jnp.where(valid, s[:, e * 2 * BLOCK:(e + 1) * 2 * BLOCK], -jnp.inf)
                sink = sink_ref[2 * pair + e]
                m = jnp.maximum(jnp.max(se, axis=-1, keepdims=True), sink)
                ex = jnp.exp(se - m)
                den = jnp.sum(ex, axis=-1, keepdims=True) + jnp.exp(sink - m)
                probs.append((ex / den).astype(BF16))
            o = jnp.dot(jnp.concatenate(probs, axis=1), vblk, preferred_element_type=F32)
            o_ref[:, ps_] = (o * g_ref[:, ps_].astype(F32)).astype(o_ref.dtype)


def _swa(plain, q_col, k_col, v_col, gates, gate_col, sinks, batch, seq_len):
    n = plain.shape[0]
    w = SWA_WIDTH
    nb = seq_len // BLOCK
    kvw = 2 * SWA_KV_HEADS * SWA_HEAD
    cur = lambda col, width: (lambda b, i: (b * nb + i, col // width))
    prev = lambda col, width: (lambda b, i: (b * nb + jnp.maximum(i - 1, 0), col // width))
    return pl.pallas_call(
        _swa_kernel,
        out_shape=jax.ShapeDtypeStruct((n, w), BF16),
        grid=(batch, nb),
        in_specs=[pl.BlockSpec(memory_space=pltpu.SMEM),
                  pl.BlockSpec((BLOCK, w), cur(q_col, w)),
                  pl.BlockSpec((BLOCK, kvw), prev(k_col, kvw)), pl.BlockSpec((BLOCK, kvw), cur(k_col, kvw)),
                  pl.BlockSpec((BLOCK, kvw), prev(v_col, kvw)), pl.BlockSpec((BLOCK, kvw), cur(v_col, kvw)),
                  pl.BlockSpec((BLOCK, w), cur(gate_col, w))],
        out_specs=pl.BlockSpec((BLOCK, w), cur(0, w)),
        compiler_params=_params("parallel", "parallel"),
        name="swa",
    )(sinks, plain, plain, plain, plain, plain, gates)


def _xattn_kernel(q_ref, kv_ref, g_ref, o_ref):
    for h in range(XA_HEADS):
        hs = slice(h * XA_HEAD, (h + 1) * XA_HEAD)
        vs = slice(XA_WIDTH + h * XA_HEAD, XA_WIDTH + (h + 1) * XA_HEAD)
        s = lax.dot_general(q_ref[:, hs], kv_ref[:, hs], _NT, preferred_element_type=F32)
        m = jnp.max(s, axis=-1, keepdims=True)
        ex = jnp.exp(s - m)
        p = (ex / jnp.sum(ex, axis=-1, keepdims=True)).astype(BF16)
        o = jnp.dot(p, kv_ref[:, vs], preferred_element_type=F32)
        o_ref[:, hs] = (o * g_ref[:, hs].astype(F32)).astype(o_ref.dtype)


def _xattn(plain, q_col, kv, gates, gate_col, batch, seq_len, tq):
    n = plain.shape[0]
    w = XA_WIDTH
    tq = min(tq, seq_len)
    nq = seq_len // tq
    m = kv.shape[0] // batch
    tok = lambda col: pl.BlockSpec((tq, w), lambda b, i: (b * nq + i, col // w))
    return pl.pallas_call(
        _xattn_kernel,
        out_shape=jax.ShapeDtypeStruct((n, w), BF16),
        grid=(batch, nq),
        in_specs=[tok(q_col), pl.BlockSpec((m, kv.shape[1]), lambda b, i: (b, 0)), tok(gate_col)],
        out_specs=tok(0),
        compiler_params=_params("parallel", "parallel"),
        name="xattn",
    )(plain, kv, gates)


def _merge_kernel(ya_ref, yb_ref, yc_ref, ga_ref, gb_ref, gc_ref, wa_ref, wb_ref, wc_ref, o_ref):
    d = lambda y, w: jnp.dot(y[...], w[...], preferred_element_type=F32)
    acc = ga_ref[...].astype(F32) * d(ya_ref, wa_ref)
    acc += gb_ref[...].astype(F32) * d(yb_ref, wb_ref)
    acc += gc_ref[...].astype(F32) * d(yc_ref, wc_ref)
    o_ref[...] = acc.astype(o_ref.dtype)


def _merge(ya, yb, yc, gates, wa, wb, wc, tm, tn):
    n = ya.shape[0]
    d = wa.shape[1]
    tm = min(tm, n)
    nj = d // tn
    y_spec = lambda a: pl.BlockSpec((tm, a.shape[1]), lambda i, j: (i, 0))
    w_spec = lambda a: pl.BlockSpec((a.shape[0], tn), lambda i, j: (0, j))
    g_spec = lambda br: pl.BlockSpec((tm, tn), lambda i, j: (i, br * nj + j))
    return pl.pallas_call(
        _merge_kernel,
        out_shape=jax.ShapeDtypeStruct((n, d), BF16),
        grid=(n // tm, nj),
        in_specs=[y_spec(ya), y_spec(yb), y_spec(yc), g_spec(0), g_spec(1), g_spec(2),
                  w_spec(wa), w_spec(wb), w_spec(wc)],
        out_specs=pl.BlockSpec((tm, tn), lambda i, j: (i, j)),
        compiler_params=_params("parallel", "arbitrary"),
        name="merge",
    )(ya, yb, yc, gates, gates, gates, wa, wb, wc)


def _out_kernel(m_ref, w_ref, x_ref, g_ref, o_ref):
    o = jnp.dot(m_ref[...], w_ref[...], preferred_element_type=F32)
    ms = jnp.mean(o * o, axis=-1, keepdims=True)
    o_ref[...] = x_ref[...] + o * lax.rsqrt(ms + NORM_EPS) * g_ref[...]


def _out_proj(merged, w_out, x2d, g_post, tm):
    n, d = x2d.shape
    tm = min(tm, n)
    tok = pl.BlockSpec((tm, d), lambda i: (i, 0))
    return pl.pallas_call(
        _out_kernel,
        out_shape=jax.ShapeDtypeStruct((n, d), F32),
        grid=(n // tm,),
        in_specs=[tok, pl.BlockSpec((d, d), lambda i: (0, 0)), tok,
                  pl.BlockSpec((1, d), lambda i: (0, 0))],
        out_specs=tok,
        compiler_params=_params("parallel"),
        name="out_proj",
    )(merged, w_out, x2d, g_post.reshape(1, d))


def _dup_heads(wcols):
    d, w = wcols.shape
    h = w // SWA_HEAD
    return jnp.broadcast_to(wcols.reshape(d, h, 1, SWA_HEAD), (d, h, 2, SWA_HEAD)).reshape(d, 2 * w)


def kernel(x, mem, g_pre, w_in, mu_shift, decay_base, decay_up, iclr_base, iclr_up, k_k, k_a, r_k,
           gn_w, gn_b, attn_sinks, g_mem, w_mem_kv, w_up_rwkv, w_up_swa, w_up_xattn, w_out, g_post):
    batch, seq_len, d = x.shape
    n = batch * seq_len
    x2d = x.reshape(n, d)
    mem2d = mem.reshape(batch * mem.shape[1], d)
    w = RWKV_WIDTH
    kvh = SWA_KV_HEADS * SWA_HEAD
    o_gate_a = SHIFT_COLS
    o_q_b = o_gate_a + w
    o_kv_b = o_q_b + SWA_WIDTH
    o_gate_b = o_kv_b + 2 * kvh
    o_q_c = o_gate_b + SWA_WIDTH
    o_gate_c = o_q_c + XA_WIDTH
    o_merge = o_gate_c + XA_WIDTH

    for l in range(DEPTH):
        wl = w_in[l]
        w_shift = wl[:, :SHIFT_COLS].astype(BF16)
        w_gates = jnp.concatenate([wl[:, o_gate_a:o_q_b], wl[:, o_gate_b:o_q_c],
                                   wl[:, o_gate_c:o_merge]], axis=1).astype(BF16)
        w_plain = jnp.concatenate(
            [wl[:, o_q_b:o_kv_b] * (SWA_HEAD ** -0.5), wl[:, o_q_c:o_gate_c] * (XA_HEAD ** -0.5),
             _dup_heads(wl[:, o_kv_b:o_gate_b])], axis=1).astype(BF16)
        w_merge = wl[:, o_merge:].astype(BF16)
        wda = jnp.zeros((LANES, 2 * w), F32)
        wda = wda.at[:DECAY_RANK, :w].set(decay_up[l]).at[DECAY_RANK:, w:].set(iclr_up[l])

        h = _rmsnorm(x2d, g_pre[l], 512)
        p_shift = _matmul(h, w_shift, act=None, out_dtype=F32, tm=1024, tn=640, name="proj_shift")
        p_gates = _matmul(h, w_gates, act="silu", out_dtype=BF16, tm=1024, tn=1024, name="proj_gates")
        p_plain = _matmul(h, w_plain, act=None, out_dtype=BF16, tm=1024, tn=512, name="proj_plain")
        p_merge = _matmul(h, w_merge, act="sigmoid", out_dtype=BF16, tm=1024, tn=1024, name="proj_merge")

        k_col = SWA_WIDTH + XA_WIDTH
        prep = _rwkv_prep(p_shift, seq_len, mu_shift[l], wda, decay_base[l], iclr_base[l],
                          k_k[l], k_a[l], r_k[l].reshape(-1), 256)
        y_a = _rwkv_chunk(prep, p_gates, 0, gn_w[l], gn_b[l], batch, seq_len)
        y_b = _swa(p_plain, 0, k_col, k_col + 2 * kvh, p_gates, w, attn_sinks[l], batch, seq_len)

        mem_n = _rmsnorm(mem2d, g_mem[l], 512)
        kv_c = _matmul(mem_n, w_mem_kv[l].astype(BF16), act=None, out_dtype=BF16, tm=1024, tn=1024,
                       name="mem_kv")
        y_c = _xattn(p_plain, SWA_WIDTH, kv_c, p_gates, w + SWA_WIDTH, batch, seq_len, 512)

        merged = _merge(y_a, y_b, y_c, p_merge, w_up_rwkv[l].astype(BF16), w_up_swa[l].astype(BF16),
                        w_up_xattn[l].astype(BF16), 1024, 512)
        x2d = _out_proj(merged, w_out[l].astype(BF16), x2d, g_post[l], 512)
    return x2d.reshape(batch, seq_len, d)
```

```python
import functools

import jax
import jax.numpy as jnp
import numpy as np
from jax import lax
from jax.experimental import pallas as pl
from jax.experimental.pallas import tpu as pltpu

F32 = jnp.float32
BF16 = jnp.bfloat16

D_MODEL = 2048
DEPTH = 2
MEM_LEN = 256
NORM_EPS = 1e-6
RWKV_WIDTH = 1024
RWKV_HEAD = 64
DECAY_RANK = 64
ICLR_RANK = 64
GN_EPS = 64e-5
SWA_HEAD = 64
SWA_Q_HEADS = 16
SWA_KV_HEADS = 2
SWA_WIDTH = SWA_Q_HEADS * SWA_HEAD
BLOCK = 128
XA_HEADS = 4
XA_HEAD = 256
XA_WIDTH = XA_HEADS * XA_HEAD
N_BRANCH = 3
SHIFT_COLS = 3 * RWKV_WIDTH + DECAY_RANK + ICLR_RANK

LANES = 128
CHUNK = 64
PAIR = 2 * RWKV_HEAD
VMEM_LIMIT = 56 * 1024 * 1024

_NT = (((1,), (1,)), ((), ()))
_TN = (((0,), (0,)), ((), ()))


def _params(*sem):
    return pltpu.CompilerParams(dimension_semantics=sem, vmem_limit_bytes=VMEM_LIMIT)


def _sigmoid(x):
    return 1.0 / (1.0 + jnp.exp(-x))


def _split3(x):
    hi = x.astype(BF16)
    r1 = x - hi.astype(F32)
    mid = r1.astype(BF16)
    lo = (r1 - mid.astype(F32)).astype(BF16)
    return hi, mid, lo


def _dot_exact_rhs(x, m):
    hi, mid, lo = _split3(x)
    d = lambda a: jnp.dot(a, m, preferred_element_type=F32)
    return d(hi) + d(mid) + d(lo)


def _dot_exact_lhs(m, x):
    hi, mid, lo = _split3(x)
    d = lambda a: jnp.dot(m, a, preferred_element_type=F32)
    return d(hi) + d(mid) + d(lo)


def _rmsnorm_kernel(x_ref, g_ref, o_ref):
    xf = x_ref[...]
    ms = jnp.mean(xf * xf, axis=-1, keepdims=True)
    o_ref[...] = (xf * lax.rsqrt(ms + NORM_EPS) * g_ref[...]).astype(o_ref.dtype)


def _rmsnorm(x2d, g, tm):
    m, d = x2d.shape
    tm = min(tm, m)
    return pl.pallas_call(
        _rmsnorm_kernel,
        out_shape=jax.ShapeDtypeStruct((m, d), BF16),
        grid=(m // tm,),
        in_specs=[pl.BlockSpec((tm, d), lambda i: (i, 0)),
                  pl.BlockSpec((1, d), lambda i: (0, 0))],
        out_specs=pl.BlockSpec((tm, d), lambda i: (i, 0)),
        compiler_params=_params("parallel"),
        name="rmsnorm",
    )(x2d, g.reshape(1, d))


def _mm_kernel(h_ref, w_ref, o_ref, *, act):
    acc = jnp.dot(h_ref[...], w_ref[...], preferred_element_type=F32)
    if act == "silu":
        acc = acc * _sigmoid(acc)
    elif act == "sigmoid":
        acc = _sigmoid(acc)
    o_ref[...] = acc.astype(o_ref.dtype)


def _matmul(h, w, *, act, out_dtype, tm, tn, name):
    m, k = h.shape
    n = w.shape[1]
    tm = min(tm, m)
    return pl.pallas_call(
        functools.partial(_mm_kernel, act=act),
        out_shape=jax.ShapeDtypeStruct((m, n), out_dtype),
        grid=(m // tm, n // tn),
        in_specs=[pl.BlockSpec((tm, k), lambda i, j: (i, 0)),
                  pl.BlockSpec((k, tn), lambda i, j: (0, j))],
        out_specs=pl.BlockSpec((tm, tn), lambda i, j: (i, j)),
        compiler_params=_params("parallel", "arbitrary"),
        name=name,
    )(h, w)


def _rwkv_prep_kernel(p_ref, prev_ref, mu_ref, wst_ref, wlo_ref, dbase_ref, ibase_ref, kk_ref, ka_ref,
                      rk_ref, tri_ref, ones_ref,
                      at_ref, rt_ref, bt_ref, kt_ref, bd_ref, kd_ref, v_ref, bonus_ref, wc_ref,
                      *, blocks_per_seq):
    tp = p_ref.shape[0]
    w = RWKV_WIDTH
    gw = 2 * LANES
    first = (pl.program_id(0) % blocks_per_seq) == 0

    def shifted(c0, width):
        cols = slice(c0, c0 + width)
        x = p_ref[:, cols]
        xs = pltpu.roll(x, 1, 0)
        prev_last = jnp.where(first, 0.0, prev_ref[7:8, cols])
        row = lax.broadcasted_iota(jnp.int32, (8, width), 0)
        xprev = jnp.concatenate([jnp.where(row == 0, prev_last, xs[:8]), xs[8:]], axis=0)
        return x + mu_ref[:, cols] * (xprev - x)

    def dot(a, b):
        return jnp.dot(a, b, preferred_element_type=F32)

    def split2(x):
        hi = x.astype(BF16)
        return hi, (x - hi.astype(F32)).astype(BF16)

    def head_sum(t):
        hi, lo = split2(t)
        return dot(hi, ones_ref[...]) + dot(lo, ones_ref[...])

    z = shifted(3 * w, LANES)
    lane = lax.broadcasted_iota(jnp.int32, z.shape, 1)
    z = jnp.where(lane < DECAY_RANK, jnp.tanh(z), z)
    z_hi, z_lo = split2(z)
    zz = jnp.concatenate([z_hi, z_lo], axis=1)

    def up_proj(c0):
        cols = slice(c0, c0 + gw)
        return dot(zz, wst_ref[:, cols]) + dot(z_hi, wlo_ref[:, cols])

    for g in range(w // gw):
        c0 = g * gw
        cols = slice(c0, c0 + gw)
        r = shifted(c0, gw)
        k = shifted(w + c0, gw)
        v = shifted(2 * w + c0, gw)
        nz = -(dbase_ref[:, cols] + up_proj(c0))
        a = _sigmoid(ibase_ref[:, cols] + up_proj(w + c0))
        softplus = jnp.maximum(nz, 0.0) + jnp.log(1.0 + jnp.exp(-jnp.abs(nz)))
        logw = -jnp.exp(-softplus - 0.5)

        cs = _dot_exact_lhs(tri_ref[...], logw)
        lasts = [cs[c * CHUNK + CHUNK - 1:c * CHUNK + CHUNK, :] for c in range(tp // CHUNK)]
        cs_last = jnp.concatenate([jnp.broadcast_to(t, (CHUNK, gw)) for t in lasts], axis=0)

        kk = k * kk_ref[:, cols]
        kk = kk * jnp.minimum(lax.rsqrt(head_sum(kk * kk)), 1e12)
        kp = k * (1.0 + (a - 1.0) * ka_ref[:, cols])
        b = kk * a
        e_neg = jnp.exp(-cs)
        e_d = jnp.exp(cs_last - cs)
        at_ref[:, cols] = (-kk * jnp.exp(cs - logw)).astype(BF16)
        rt_ref[:, cols] = (r * jnp.exp(cs)).astype(BF16)
        bt_ref[:, cols] = (b * e_neg).astype(BF16)
        kt_ref[:, cols] = (kp * e_neg).astype(BF16)
        bd_ref[:, cols] = (b * e_d).astype(BF16)
        kd_ref[:, cols] = (kp * e_d).astype(BF16)
        v_ref[:, cols] = v.astype(BF16)
        bonus_ref[:, cols] = head_sum(r * kp * rk_ref[:, cols]) * v
        for c, t in enumerate(lasts):
            wc_ref[0, c:c + 1, cols] = jnp.exp(t)


def _rwkv_prep(p, seq_len, mu, wda, dbase, ibase, k_k, k_a, r_k, tp):
    n = p.shape[0]
    w = RWKV_WIDTH
    tp = min(tp, seq_len)
    nblk = n // tp
    cpb = tp // CHUNK
    tri = np.tril(np.ones((CHUNK, CHUNK), np.float32))
    tri = jnp.asarray(np.kron(np.eye(cpb, dtype=np.float32), tri), BF16)
    ones_bd = jnp.asarray(np.kron(np.eye(2 * LANES // RWKV_HEAD, dtype=np.float32),
                                  np.ones((RWKV_HEAD, RWKV_HEAD), np.float32)), BF16)
    w_hi = wda.astype(BF16)
    w_lo = (wda - w_hi.astype(F32)).astype(BF16)
    w_st = jnp.concatenate([w_hi, w_hi], axis=0)
    row = lambda a: a.reshape(1, -1)
    full = lambda shape: pl.BlockSpec(shape, lambda i: (0,) * len(shape))
    tok = pl.BlockSpec((tp, w), lambda i: (i, 0))
    outs = pl.pallas_call(
        functools.partial(_rwkv_prep_kernel, blocks_per_seq=seq_len // tp),
        out_shape=[jax.ShapeDtypeStruct((n, w), BF16)] * 7
        + [jax.ShapeDtypeStruct((n, w), F32), jax.ShapeDtypeStruct((nblk, cpb, w), F32)],
        grid=(nblk,),
        in_specs=[pl.BlockSpec((tp, SHIFT_COLS), lambda i: (i, 0)),
                  pl.BlockSpec((8, SHIFT_COLS), lambda i: (jnp.maximum(i * (tp // 8) - 1, 0), 0)),
                  full((1, SHIFT_COLS)), full((2 * LANES, 2 * w)), full((LANES, 2 * w)),
                  full((1, w)), full((1, w)), full((1, w)), full((1, w)), full((1, w)),
                  full((tp, tp)), full((2 * LANES, 2 * LANES))],
        out_specs=[tok] * 8 + [pl.BlockSpec((1, cpb, w), lambda i: (i, 0, 0))],
        compiler_params=_params("parallel"),
        name="rwkv_prep",
    )(p, p, row(mu), w_st, w_lo, row(dbase), row(ibase), row(k_k), row(k_a), row(r_k), tri, ones_bd)
    return outs


def _rwkv_chunk_kernel(at_ref, rt_ref, bt_ref, kt_ref, bd_ref, kd_ref, v_ref, bonus_ref, wc_ref,
                       gate_ref, gnw_ref, gnb_ref, avg_ref, o_ref,
                       pt_ref, qt_ref, g_ref, y0_ref, y_ref, *, n_chunks, n_pairs):
    c2 = 2 * CHUNK
    lane = lax.broadcasted_iota(jnp.int32, (CHUNK, PAIR), 1)
    head0 = lane < RWKV_HEAD
    ri = lax.broadcasted_iota(jnp.int32, (2 * c2, 2 * c2), 0)
    ci = lax.broadcasted_iota(jnp.int32, (2 * c2, 2 * c2), 1)
    keep = (ci & (CHUNK - 1)) < (ri & (CHUNK - 1)) + jnp.where(ri < c2, 0, 1)
    eye = (lax.broadcasted_iota(jnp.int32, (c2, c2), 0)
           == lax.broadcasted_iota(jnp.int32, (c2, c2), 1)).astype(F32)
    zeros_b = jnp.zeros((c2, c2), BF16)

    def stacked(ref, sl, ls):
        x = ref[sl, ls]
        zero = jnp.zeros_like(x)
        return jnp.concatenate([jnp.where(head0, x, zero), jnp.where(head0, zero, x)], axis=0)

    def dot(a, b):
        return jnp.dot(a, b, preferred_element_type=F32)

    def phase1_stages(items):
        def ld(ref, it):
            p, c = it
            return stacked(ref, pl.ds(pl.multiple_of(c * CHUNK, CHUNK), CHUNK),
                           slice(p * PAIR, (p + 1) * PAIR))

        lms = [lax.dot_general(jnp.concatenate([ld(at_ref, it), ld(rt_ref, it)], axis=0),
                               jnp.concatenate([ld(bt_ref, it), ld(kt_ref, it)], axis=0), _NT,
                               preferred_element_type=F32) for it in items]
        lms = [jnp.where(keep, lm, 0.0) for lm in lms]
        lak_bs = [lm[:c2, c2:].astype(BF16) for lm in lms]
        m_rs = [lm[c2:, :].astype(BF16) for lm in lms]
        yield
        invs = [eye + lm[:c2, :c2] for lm in lms]
        l_bs = [lm[:c2, :c2].astype(BF16) for lm in lms]
        lps = [dot(l_b, l_b) for l_b in l_bs]
        lak_vs = [dot(lak_b, ld(v_ref, it)).astype(BF16) for lak_b, it in zip(lak_bs, items)]
        yield
        n_fac = int(np.log2(CHUNK)) - 1
        for f in range(n_fac - 1):
            lp_bs = [lp.astype(BF16) for lp in lps]
            prs = [dot(lp_b, jnp.concatenate([lp_b, inv.astype(BF16)], axis=1))
                   for lp_b, inv in zip(lp_bs, invs)]
            lps = [pr[:, :c2] for pr in prs]
            invs = [inv + pr[:, c2:] for inv, pr in zip(invs, prs)]
            yield
        invs = [inv + dot(lp.astype(BF16), inv.astype(BF16)) for lp, inv in zip(lps, invs)]
        yield
        xu_bs = [dot(inv.astype(BF16), jnp.concatenate([ld(at_ref, it), lak_v], axis=1)).astype(BF16)
                 for inv, lak_v, it in zip(invs, lak_vs, items)]
        rhss = [jnp.concatenate([xu_b, jnp.concatenate([zeros_b, ld(v_ref, it)], axis=1)], axis=0)
                for xu_b, it in zip(xu_bs, items)]
        yield
        gys = [dot(m_r, rhs) for m_r, rhs in zip(m_rs, rhss)]
        yield
        pqs = [lax.dot_general(rhs, jnp.concatenate([ld(bd_ref, it), ld(kd_ref, it)], axis=0), _TN,
                               preferred_element_type=F32)
               for rhs, it in zip(rhss, items)]
        for (p, c), gy, pq in zip(items, gys, pqs):
            g_ref[p, c] = (ld(rt_ref, (p, c)).astype(F32) + gy[:, :c2]).astype(BF16)
            y0_ref[p, c] = gy[:, c2:]
            pt_ref[p, c] = pq[:c2, :].astype(BF16)
            qt_ref[p, c] = pq[c2:, :]

    def phase2_chunk(c, sts):
        sl = pl.ds(pl.multiple_of(c * CHUNK, CHUNK), CHUNK)
        new = []
        for p in range(n_pairs):
            ls = slice(p * PAIR, (p + 1) * PAIR)
            st = sts[p]
            st_b = st.astype(BF16)
            y_d = lax.dot_general(g_ref[p, c], st_b, _NT, preferred_element_type=F32) + y0_ref[p, c]
            y_ref[sl, ls] = y_d[:CHUNK, :] + y_d[CHUNK:, :]
            new.append(st * wc_ref[0, c, :, ls] + dot(st_b, pt_ref[p, c]) + qt_ref[p, c])
        return tuple(new)

    group = 4
    n_groups = n_chunks // group

    def run_group(i, sts, build, scan):
        stages = (phase1_stages([(p, i * group + u) for p in range(n_pairs) for u in range(group)])
                  if build else iter(()))
        todo = [(i - 1) * group + u for u in range(group)] if scan else []
        for s, _ in enumerate(stages):
            if todo and s % 2 == 1:
                sts = phase2_chunk(todo.pop(0), sts)
        for c in todo:
            sts = phase2_chunk(c, sts)
        return sts

    sts = tuple(jnp.zeros((c2, c2), F32) for _ in range(n_pairs))
    sts = run_group(0, sts, True, False)
    sts = lax.fori_loop(1, n_groups, lambda i, s: run_group(i, s, True, True), sts)
    run_group(n_groups, sts, False, True)

    rows3 = 4 * CHUNK
    avg = avg_ref[...]

    unroll3 = 2

    def phase3(i, carry):
        items = [(pl.ds(pl.multiple_of((i * unroll3 + u) * rows3, rows3), rows3),
                  slice(p * PAIR, (p + 1) * PAIR)) for u in range(unroll3) for p in range(n_pairs)]
        ys = [y_ref[sl, ls] for sl, ls in items]
        ds = [y - _dot_exact_rhs(y, avg) for y in ys]
        vs = [_dot_exact_rhs(d * d, avg) for d in ds]
        for (sl, ls), d, var in zip(items, ds, vs):
            yn = d * lax.rsqrt(var + GN_EPS) * gnw_ref[:, ls] + gnb_ref[:, ls]
            o_ref[sl, ls] = ((yn + bonus_ref[sl, ls]) * gate_ref[sl, ls].astype(F32)).astype(o_ref.dtype)
        return carry

    lax.fori_loop(0, n_chunks * CHUNK // (rows3 * unroll3), phase3, 0)


def _rwkv_chunk(prep, gates, gate_col, gn_w, gn_b, batch, seq_len):
    at, rt, bt, kt, bd, kd, v, bonus, wc = prep
    n, w = at.shape
    n_chunks = seq_len // CHUNK
    n_pairs = 2
    bw = n_pairs * PAIR
    wc = wc.reshape(batch, n_chunks, 1, w)
    avg = jnp.asarray(np.kron(np.eye(PAIR // RWKV_HEAD, dtype=np.float32),
                              np.full((RWKV_HEAD, RWKV_HEAD), 1.0 / RWKV_HEAD, np.float32)), BF16)
    tok = pl.BlockSpec((seq_len, bw), lambda b, h: (b, h))
    gate_spec = pl.BlockSpec((seq_len, bw), lambda b, h: (b, gate_col // bw + h))
    vec = pl.BlockSpec((1, bw), lambda b, h: (0, h))
    c2 = 2 * CHUNK
    ops = lambda dt: pltpu.VMEM((n_pairs, n_chunks, c2, c2), dt)
    return pl.pallas_call(
        functools.partial(_rwkv_chunk_kernel, n_chunks=n_chunks, n_pairs=n_pairs),
        out_shape=jax.ShapeDtypeStruct((n, w), BF16),
        grid=(batch, w // bw),
        in_specs=[tok] * 8 + [pl.BlockSpec((1, n_chunks, 1, bw), lambda b, h: (b, 0, 0, h)),
                              gate_spec, vec, vec,
                              pl.BlockSpec((PAIR, PAIR), lambda b, h: (0, 0))],
        out_specs=tok,
        scratch_shapes=[ops(BF16), ops(F32), ops(BF16), ops(F32), pltpu.VMEM((seq_len, bw), F32)],
        compiler_params=_params("parallel", "parallel"),
        name="rwkv_chunk",
    )(at, rt, bt, kt, bd, kd, v, bonus, wc, gates, gn_w.reshape(1, w), gn_b.reshape(1, w), avg)


def _swa_kernel(sink_ref, q_ref, kp_ref, kc_ref, vp_ref, vc_ref, g_ref, o_ref):
    n = pl.program_id(1)
    lane = lax.broadcasted_iota(jnp.int32, (2 * BLOCK, LANES), 1)
    head0 = lane < SWA_HEAD
    head0_q = lax.broadcasted_iota(jnp.int32, (BLOCK, LANES), 1) < SWA_HEAD
    r = lax.broadcasted_iota(jnp.int32, (BLOCK, 2 * BLOCK), 0)
    c = lax.broadcasted_iota(jnp.int32, (BLOCK, 2 * BLOCK), 1)
    valid = (c > r) & (c <= r + BLOCK) & ((c >= BLOCK) | (n > 0))
    pairs_per_kv = SWA_Q_HEADS // SWA_KV_HEADS // 2

    def block_diag(prev, cur):
        x = jnp.concatenate([prev, cur], axis=0)
        zero = jnp.zeros_like(x)
        return jnp.concatenate([jnp.where(head0, x, zero), jnp.where(head0, zero, x)], axis=0)

    for h in range(SWA_KV_HEADS):
        hs = slice(h * LANES, (h + 1) * LANES)
        kblk = block_diag(kp_ref[:, hs], kc_ref[:, hs])
        vblk = block_diag(vp_ref[:, hs], vc_ref[:, hs])
        pairs = [h * pairs_per_kv + j for j in range(pairs_per_kv)]
        cols = [slice(pair * LANES, (pair + 1) * LANES) for pair in pairs]
        ss = [lax.dot_general(q_ref[:, cs_], kblk, _NT, preferred_element_type=F32) for cs_ in cols]
        heads = [(i, e) for i in range(len(pairs)) for e in range(2)]
        tiles = [jnp.where(valid, ss[i][:, e * 2 * BLOCK:(e + 1) * 2 * BLOCK], -jnp.inf)
                 for i, e in heads]
        sinks = [sink_ref[2 * pairs[i] + e] for i, e in heads]
        ms = [jnp.maximum(jnp.max(t, axis=-1, keepdims=True), sk) for t, sk in zip(tiles, sinks)]
        exs = [jnp.exp(t - m) for t, m in zip(tiles, ms)]
        dens = [jnp.sum(ex, axis=-1, keepdims=True) + jnp.exp(sk - m)
                for ex, sk, m in zip(exs, sinks, ms)]
        outs = [jnp.dot(jnp.concatenate([exs[2 * i].astype(BF16), exs[2 * i + 1].astype(BF16)], axis=1),
                        vblk, preferred_element_type=F32) for i in range(len(pairs))]
        for i, cs_ in enumerate(cols):
            rden = jnp.where(head0_q, 1.0 / dens[2 * i], 1.0 / dens[2 * i + 1])
            o_ref[:, cs_] = (outs[i] * rden * g_ref[:, cs_].astype(F32)).astype(o_ref.dtype)


def _swa(plain, q_col, k_col, v_col, gates, gate_col, sinks, batch, seq_len):
    n = plain.shape[0]
    w = SWA_WIDTH
    nb = seq_len // BLOCK
    kvw = 2 * SWA_KV_HEADS * SWA_HEAD
    cur = lambda col, width: (lambda b, i: (b * nb + i, col // width))
    prev = lambda col, width: (lambda b, i: (b * nb + jnp.maximum(i - 1, 0), col // width))
    return pl.pallas_call(
        _swa_kernel,
        out_shape=jax.ShapeDtypeStruct((n, w), BF16),
        grid=(batch, nb),
        in_specs=[pl.BlockSpec(memory_space=pltpu.SMEM),
                  pl.BlockSpec((BLOCK, w), cur(q_col, w)),
                  pl.BlockSpec((BLOCK, kvw), prev(k_col, kvw)), pl.BlockSpec((BLOCK, kvw), cur(k_col, kvw)),
                  pl.BlockSpec((BLOCK, kvw), prev(v_col, kvw)), pl.BlockSpec((BLOCK, kvw), cur(v_col, kvw)),
                  pl.BlockSpec((BLOCK, w), cur(gate_col, w))],
        out_specs=pl.BlockSpec((BLOCK, w), cur(0, w)),
        compiler_params=_params("parallel", "parallel"),
        name="swa",
    )(sinks, plain, plain, plain, plain, plain, gates)


def _xattn_kernel(q_ref, kv_ref, g_ref, o_ref):
    for h in range(XA_HEADS):
        hs = slice(h * XA_HEAD, (h + 1) * XA_HEAD)
        vs = slice(XA_WIDTH + h * XA_HEAD, XA_WIDTH + (h + 1) * XA_HEAD)
        s = lax.dot_general(q_ref[:, hs], kv_ref[:, hs], _NT, preferred_element_type=F32)
        m = jnp.max(s, axis=-1, keepdims=True)
        ex = jnp.exp(s - m)
        p = (ex / jnp.sum(ex, axis=-1, keepdims=True)).astype(BF16)
        o = jnp.dot(p, kv_ref[:, vs], preferred_element_type=F32)
        o_ref[:, hs] = (o * g_ref[:, hs].astype(F32)).astype(o_ref.dtype)


def _xattn(plain, q_col, kv, gates, gate_col, batch, seq_len, tq):
    n = plain.shape[0]
    w = XA_WIDTH
    tq = min(tq, seq_len)
    nq = seq_len // tq
    m = kv.shape[0] // batch
    tok = lambda col: pl.BlockSpec((tq, w), lambda b, i: (b * nq + i, col // w))
    return pl.pallas_call(
        _xattn_kernel,
        out_shape=jax.ShapeDtypeStruct((n, w), BF16),
        grid=(batch, nq),
        in_specs=[tok(q_col), pl.BlockSpec((m, kv.shape[1]), lambda b, i: (b, 0)), tok(gate_col)],
        out_specs=tok(0),
        compiler_params=_params("parallel", "parallel"),
        name="xattn",
    )(plain, kv, gates)


def _merge_kernel(ya_ref, yb_ref, yc_ref, ga_ref, gb_ref, gc_ref, wa_ref, wb_ref, wc_ref, o_ref):
    d = lambda y, w: jnp.dot(y[...], w[...], preferred_element_type=F32)
    acc = ga_ref[...].astype(F32) * d(ya_ref, wa_ref)
    acc += gb_ref[...].astype(F32) * d(yb_ref, wb_ref)
    acc += gc_ref[...].astype(F32) * d(yc_ref, wc_ref)
    o_ref[...] = acc.astype(o_ref.dtype)


def _merge(ya, yb, yc, gates, wa, wb, wc, tm, tn):
    n = ya.shape[0]
    d = wa.shape[1]
    tm = min(tm, n)
    nj = d // tn
    y_spec = lambda a: pl.BlockSpec((tm, a.shape[1]), lambda i, j: (i, 0))
    w_spec = lambda a: pl.BlockSpec((a.shape[0], tn), lambda i, j: (0, j))
    g_spec = lambda br: pl.BlockSpec((tm, tn), lambda i, j: (i, br * nj + j))
    return pl.pallas_call(
        _merge_kernel,
        out_shape=jax.ShapeDtypeStruct((n, d), BF16),
        grid=(n // tm, nj),
        in_specs=[y_spec(ya), y_spec(yb), y_spec(yc), g_spec(0), g_spec(1), g_spec(2),
                  w_spec(wa), w_spec(wb), w_spec(wc)],
        out_specs=pl.BlockSpec((tm, tn), lambda i, j: (i, j)),
        compiler_params=_params("parallel", "arbitrary"),
        name="merge",
    )(ya, yb, yc, gates, gates, gates, wa, wb, wc)


def _out_kernel(m_ref, w_ref, x_ref, g_ref, o_ref):
    o = jnp.dot(m_ref[...], w_ref[...], preferred_element_type=F32)
    ms = jnp.mean(o * o, axis=-1, keepdims=True)
    o_ref[...] = x_ref[...] + o * lax.rsqrt(ms + NORM_EPS) * g_ref[...]


def _out_proj(merged, w_out, x2d, g_post, tm):
    n, d = x2d.shape
    tm = min(tm, n)
    tok = pl.BlockSpec((tm, d), lambda i: (i, 0))
    return pl.pallas_call(
        _out_kernel,
        out_shape=jax.ShapeDtypeStruct((n, d), F32),
        grid=(n // tm,),
        in_specs=[tok, pl.BlockSpec((d, d), lambda i: (0, 0)), tok,
                  pl.BlockSpec((1, d), lambda i: (0, 0))],
        out_specs=tok,
        compiler_params=_params("parallel"),
        name="out_proj",
    )(merged, w_out, x2d, g_post.reshape(1, d))


def _dup_heads(wcols):
    d, w = wcols.shape
    h = w // SWA_HEAD
    return jnp.broadcast_to(wcols.reshape(d, h, 1, SWA_HEAD), (d, h, 2, SWA_HEAD)).reshape(d, 2 * w)


def kernel(x, mem, g_pre, w_in, mu_shift, decay_base, decay_up, iclr_base, iclr_up, k_k, k_a, r_k,
           gn_w, gn_b, attn_sinks, g_mem, w_mem_kv, w_up_rwkv, w_up_swa, w_up_xattn, w_out, g_post):
    batch, seq_len, d = x.shape
    n = batch * seq_len
    x2d = x.reshape(n, d)
    mem2d = mem.reshape(batch * mem.shape[1], d)
    w = RWKV_WIDTH
    kvh = SWA_KV_HEADS * SWA_HEAD
    o_gate_a = SHIFT_COLS
    o_q_b = o_gate_a + w
    o_kv_b = o_q_b + SWA_WIDTH
    o_gate_b = o_kv_b + 2 * kvh
    o_q_c = o_gate_b + SWA_WIDTH
    o_gate_c = o_q_c + XA_WIDTH
    o_merge = o_gate_c + XA_WIDTH

    for l in range(DEPTH):
        wl = w_in[l]
        w_shift = wl[:, :SHIFT_COLS].astype(BF16)
        w_gates = jnp.concatenate([wl[:, o_gate_a:o_q_b], wl[:, o_gate_b:o_q_c],
                                   wl[:, o_gate_c:o_merge]], axis=1).astype(BF16)
        w_plain = jnp.concatenate(
            [wl[:, o_q_b:o_kv_b] * (SWA_HEAD ** -0.5), wl[:, o_q_c:o_gate_c] * (XA_HEAD ** -0.5),
             _dup_heads(wl[:, o_kv_b:o_gate_b])], axis=1).astype(BF16)
        w_merge = wl[:, o_merge:].astype(BF16)
        wda = jnp.zeros((LANES, 2 * w), F32)
        wda = wda.at[:DECAY_RANK, :w].set(decay_up[l]).at[DECAY_RANK:, w:].set(iclr_up[l])

        h = _rmsnorm(x2d, g_pre[l], 512)
        p_shift = _matmul(h, w_shift, act=None, out_dtype=F32, tm=512, tn=SHIFT_COLS, name="proj_shift")
        p_gates = _matmul(h, w_gates, act="silu", out_dtype=BF16, tm=1024, tn=1024, name="proj_gates")
        p_plain = _matmul(h, w_plain, act=None, out_dtype=BF16, tm=1024, tn=512, name="proj_plain")
        p_merge = _matmul(h, w_merge, act="sigmoid", out_dtype=BF16, tm=1024, tn=1024, name="proj_merge")

        k_col = SWA_WIDTH + XA_WIDTH
        prep = _rwkv_prep(p_shift, seq_len, mu_shift[l], wda, decay_base[l], iclr_base[l],
                          k_k[l], k_a[l], r_k[l].reshape(-1), 256)
        y_a = _rwkv_chunk(prep, p_gates, 0, gn_w[l], gn_b[l], batch, seq_len)
        y_b = _swa(p_plain, 0, k_col, k_col + 2 * kvh, p_gates, w, attn_sinks[l], batch, seq_len)

        mem_n = _rmsnorm(mem2d, g_mem[l], 512)
        kv_c = _matmul(mem_n, w_mem_kv[l].astype(BF16), act=None, out_dtype=BF16, tm=1024, tn=1024,
                       name="mem_kv")
        y_c = _xattn(p_plain, SWA_WIDTH, kv_c, p_gates, w + SWA_WIDTH, batch, seq_len, 512)

        merged = _merge(y_a, y_b, y_c, p_merge, w_up_rwkv[l].astype(BF16), w_up_swa[l].astype(BF16),
                        w_up_xattn[l].astype(BF16), 1024, 512)
        x2d = _out_proj(merged, w_out[l].astype(BF16), x2d, g_post[l], 512)
    return x2d.reshape(batch, seq_len, d)
```

```python
import functools

import jax
import jax.numpy as jnp
import numpy as np
from jax import lax
from jax.experimental import pallas as pl
from jax.experimental.pallas import tpu as pltpu

F32 = jnp.float32
BF16 = jnp.bfloat16

D_MODEL = 2048
DEPTH = 2
MEM_LEN = 256
NORM_EPS = 1e-6
RWKV_WIDTH = 1024
RWKV_HEAD = 64
DECAY_RANK = 64
ICLR_RANK = 64
GN_EPS = 64e-5
SWA_HEAD = 64
SWA_Q_HEADS = 16
SWA_KV_HEADS = 2
SWA_WIDTH = SWA_Q_HEADS * SWA_HEAD
BLOCK = 128
XA_HEADS = 4
XA_HEAD = 256
XA_WIDTH = XA_HEADS * XA_HEAD
N_BRANCH = 3
SHIFT_COLS = 3 * RWKV_WIDTH + DECAY_RANK + ICLR_RANK

LANES = 128
CHUNK = 64
PAIR = 2 * RWKV_HEAD
VMEM_LIMIT = 56 * 1024 * 1024

_NT = (((1,), (1,)), ((), ()))
_TN = (((0,), (0,)), ((), ()))


def _params(*sem):
    return pltpu.CompilerParams(dimension_semantics=sem, vmem_limit_bytes=VMEM_LIMIT)


def _sigmoid(x):
    return 1.0 / (1.0 + jnp.exp(-x))


def _split3(x):
    hi = x.astype(BF16)
    r1 = x - hi.astype(F32)
    mid = r1.astype(BF16)
    lo = (r1 - mid.astype(F32)).astype(BF16)
    return hi, mid, lo


def _dot_exact_rhs(x, m):
    hi, mid, lo = _split3(x)
    d = lambda a: jnp.dot(a, m, preferred_element_type=F32)
    return d(hi) + d(mid) + d(lo)


def _dot_exact_lhs(m, x):
    hi, mid, lo = _split3(x)
    d = lambda a: jnp.dot(m, a, preferred_element_type=F32)
    return d(hi) + d(mid) + d(lo)


def _rmsnorm_kernel(x_ref, g_ref, o_ref):
    xf = x_ref[...]
    ms = jnp.mean(xf * xf, axis=-1, keepdims=True)
    o_ref[...] = (xf * lax.rsqrt(ms + NORM_EPS) * g_ref[...]).astype(o_ref.dtype)


def _rmsnorm(x2d, g, tm):
    m, d = x2d.shape
    tm = min(tm, m)
    return pl.pallas_call(
        _rmsnorm_kernel,
        out_shape=jax.ShapeDtypeStruct((m, d), BF16),
        grid=(m // tm,),
        in_specs=[pl.BlockSpec((tm, d), lambda i: (i, 0)),
                  pl.BlockSpec((1, d), lambda i: (0, 0))],
        out_specs=pl.BlockSpec((tm, d), lambda i: (i, 0)),
        compiler_params=_params("parallel"),
        name="rmsnorm",
    )(x2d, g.reshape(1, d))


def _pick(j, values):
    out = values[0]
    for idx in range(1, len(values)):
        out = jnp.where(j == idx, values[idx], out)
    return out


def _w_spec(layer, k, tn, col_offs):
    assert all(off % LANES == 0 for off in col_offs)
    if len(col_offs) == 1:
        return pl.BlockSpec((pl.Element(1), pl.Element(k), pl.Element(tn)),
                            lambda j, i: (layer, 0, col_offs[0]))
    return pl.BlockSpec((pl.Element(1), pl.Element(k), pl.Element(tn)),
                        lambda j, i: (layer, 0, pl.multiple_of(_pick(j, col_offs), LANES)))


def _proj_kernel(h_ref, w_ref, o_ref, wb_ref, *, act, scales):
    j = pl.program_id(0)

    @pl.when(pl.program_id(1) == 0)
    def _():
        wf = w_ref[0]
        if scales is not None:
            wf = wf * _pick(j, scales)
        wb_ref[...] = wf.astype(BF16)

    acc = jnp.dot(h_ref[...], wb_ref[...], preferred_element_type=F32)
    if act == "silu":
        acc = acc * _sigmoid(acc)
    elif act == "sigmoid":
        acc = _sigmoid(acc)
    o_ref[...] = acc.astype(o_ref.dtype)


def _proj(h, w, layer, col_offs, tn, *, act, out_dtype, tm, name, scales=None):
    m, k = h.shape
    tm = min(tm, m)
    nb = len(col_offs)
    return pl.pallas_call(
        functools.partial(_proj_kernel, act=act, scales=scales),
        out_shape=jax.ShapeDtypeStruct((m, nb * tn), out_dtype),
        grid=(nb, m // tm),
        in_specs=[pl.BlockSpec((tm, k), lambda j, i: (i, 0)), _w_spec(layer, k, tn, col_offs)],
        out_specs=pl.BlockSpec((tm, tn), lambda j, i: (i, j)),
        scratch_shapes=[pltpu.VMEM((k, tn), BF16)],
        compiler_params=_params("arbitrary", "arbitrary"),
        name=name,
    )(h, w)


def _proj_kv_kernel(h_ref, w_ref, o_ref, wb_ref):
    @pl.when(pl.program_id(0) == 0)
    def _():
        wf = w_ref[0]
        head0 = lax.broadcasted_iota(jnp.int32, (wf.shape[0], LANES), 1) < SWA_HEAD
        for g in range(wf.shape[1] // LANES):
            a = wf[:, g * LANES:(g + 1) * LANES]
            ra = pltpu.roll(a, SWA_HEAD, 1)
            wb_ref[:, 2 * g * LANES:(2 * g + 1) * LANES] = jnp.where(head0, a, ra).astype(BF16)
            wb_ref[:, (2 * g + 1) * LANES:(2 * g + 2) * LANES] = jnp.where(head0, ra, a).astype(BF16)

    o_ref[...] = jnp.dot(h_ref[...], wb_ref[...], preferred_element_type=F32).astype(o_ref.dtype)


def _proj_kv(h, w, layer, col_off, tm):
    m, k = h.shape
    tm = min(tm, m)
    wi = 2 * SWA_KV_HEADS * SWA_HEAD
    return pl.pallas_call(
        _proj_kv_kernel,
        out_shape=jax.ShapeDtypeStruct((m, 2 * wi), BF16),
        grid=(m // tm,),
        in_specs=[pl.BlockSpec((tm, k), lambda i: (i, 0)),
                  pl.BlockSpec((pl.Element(1), pl.Element(k), pl.Element(wi)),
                               lambda i: (layer, 0, col_off))],
        out_specs=pl.BlockSpec((tm, 2 * wi), lambda i: (i, 0)),
        scratch_shapes=[pltpu.VMEM((k, 2 * wi), BF16)],
        compiler_params=_params("arbitrary"),
        name="proj_kv",
    )(h, w)


def _rwkv_prep_kernel(p_ref, prev_ref, z_ref, zprev_ref, mu_ref, wst_ref, wlo_ref, dbase_ref, ibase_ref, kk_ref, ka_ref,
                      rk_ref, tri_ref, ones_ref,
                      at_ref, rt_ref, bt_ref, kt_ref, bd_ref, kd_ref, v_ref, bonus_ref, wc_ref,
                      *, blocks_per_seq):
    tp = p_ref.shape[0]
    w = RWKV_WIDTH
    gw = 2 * LANES
    first = (pl.program_id(0) % blocks_per_seq) == 0

    def shifted(c0, width, cur_ref=p_ref, before_ref=prev_ref, mu0=0):
        cols = slice(c0, c0 + width)
        x = cur_ref[:, cols]
        xs = pltpu.roll(x, 1, 0)
        prev_last = jnp.where(first, 0.0, before_ref[7:8, cols])
        row = lax.broadcasted_iota(jnp.int32, (8, width), 0)
        xprev = jnp.concatenate([jnp.where(row == 0, prev_last, xs[:8]), xs[8:]], axis=0)
        return x + mu_ref[:, mu0 + c0:mu0 + c0 + width] * (xprev - x)

    def dot(a, b):
        return jnp.dot(a, b, preferred_element_type=F32)

    def split2(x):
        hi = x.astype(BF16)
        return hi, (x - hi.astype(F32)).astype(BF16)

    def head_sum(t):
        hi, lo = split2(t)
        return dot(hi, ones_ref[...]) + dot(lo, ones_ref[...])

    z = shifted(0, LANES, z_ref, zprev_ref, 3 * w)
    lane = lax.broadcasted_iota(jnp.int32, z.shape, 1)
    z = jnp.where(lane < DECAY_RANK, jnp.tanh(z), z)
    z_hi, z_lo = split2(z)
    zz = jnp.concatenate([z_hi, z_lo], axis=1)

    def up_proj(c0):
        cols = slice(c0, c0 + gw)
        return dot(zz, wst_ref[:, cols]) + dot(z_hi, wlo_ref[:, cols])

    for g in range(w // gw):
        c0 = g * gw
        cols = slice(c0, c0 + gw)
        r = shifted(c0, gw)
        k = shifted(w + c0, gw)
        v = shifted(2 * w + c0, gw)
        nz = -(dbase_ref[:, cols] + up_proj(c0))
        a = _sigmoid(ibase_ref[:, cols] + up_proj(w + c0))
        softplus = jnp.maximum(nz, 0.0) + jnp.log(1.0 + jnp.exp(-jnp.abs(nz)))
        logw = -jnp.exp(-softplus - 0.5)

        cs = _dot_exact_lhs(tri_ref[...], logw)
        lasts = [cs[c * CHUNK + CHUNK - 1:c * CHUNK + CHUNK, :] for c in range(tp // CHUNK)]
        cs_last = jnp.concatenate([jnp.broadcast_to(t, (CHUNK, gw)) for t in lasts], axis=0)

        kk = k * kk_ref[:, cols]
        kk = kk * jnp.minimum(lax.rsqrt(head_sum(kk * kk)), 1e12)
        kp = k * (1.0 + (a - 1.0) * ka_ref[:, cols])
        b = kk * a
        e_neg = jnp.exp(-cs)
        e_d = jnp.exp(cs_last - cs)
        at_ref[:, cols] = (-kk * jnp.exp(cs - logw)).astype(BF16)
        rt_ref[:, cols] = (r * jnp.exp(cs)).astype(BF16)
        bt_ref[:, cols] = (b * e_neg).astype(BF16)
        kt_ref[:, cols] = (kp * e_neg).astype(BF16)
        bd_ref[:, cols] = (b * e_d).astype(BF16)
        kd_ref[:, cols] = (kp * e_d).astype(BF16)
        v_ref[:, cols] = v.astype(BF16)
        bonus_ref[:, cols] = head_sum(r * kp * rk_ref[:, cols]) * v
        for c, t in enumerate(lasts):
            wc_ref[0, c:c + 1, cols] = jnp.exp(t)


def _rwkv_prep(p, pz, seq_len, mu, wda, dbase, ibase, k_k, k_a, r_k, tp):
    n = p.shape[0]
    w = RWKV_WIDTH
    tp = min(tp, seq_len)
    nblk = n // tp
    cpb = tp // CHUNK
    tri = np.tril(np.ones((CHUNK, CHUNK), np.float32))
    tri = jnp.asarray(np.kron(np.eye(cpb, dtype=np.float32), tri), BF16)
    ones_bd = jnp.asarray(np.kron(np.eye(2 * LANES // RWKV_HEAD, dtype=np.float32),
                                  np.ones((RWKV_HEAD, RWKV_HEAD), np.float32)), BF16)
    w_hi = wda.astype(BF16)
    w_lo = (wda - w_hi.astype(F32)).astype(BF16)
    w_st = jnp.concatenate([w_hi, w_hi], axis=0)
    row = lambda a: a.reshape(1, -1)
    full = lambda shape: pl.BlockSpec(shape, lambda i: (0,) * len(shape))
    tok = pl.BlockSpec((tp, w), lambda i: (i, 0))
    before = lambda i: (jnp.maximum(i * (tp // 8) - 1, 0), 0)
    outs = pl.pallas_call(
        functools.partial(_rwkv_prep_kernel, blocks_per_seq=seq_len // tp),
        out_shape=[jax.ShapeDtypeStruct((n, w), BF16)] * 7
        + [jax.ShapeDtypeStruct((n, w), F32), jax.ShapeDtypeStruct((nblk, cpb, w), F32)],
        grid=(nblk,),
        in_specs=[pl.BlockSpec((tp, 3 * w), lambda i: (i, 0)),
                  pl.BlockSpec((8, 3 * w), before),
                  pl.BlockSpec((tp, LANES), lambda i: (i, 0)),
                  pl.BlockSpec((8, LANES), before),
                  full((1, SHIFT_COLS)), full((2 * LANES, 2 * w)), full((LANES, 2 * w)),
                  full((1, w)), full((1, w)), full((1, w)), full((1, w)), full((1, w)),
                  full((tp, tp)), full((2 * LANES, 2 * LANES))],
        out_specs=[tok] * 8 + [pl.BlockSpec((1, cpb, w), lambda i: (i, 0, 0))],
        compiler_params=_params("parallel"),
        name="rwkv_prep",
    )(p, p, pz, pz, row(mu), w_st, w_lo, row(dbase), row(ibase), row(k_k), row(k_a), row(r_k), tri, ones_bd)
    return outs


def _rwkv_chunk_kernel(at_ref, rt_ref, bt_ref, kt_ref, bd_ref, kd_ref, v_ref, bonus_ref, wc_ref,
                       gate_ref, gnw_ref, gnb_ref, avg_ref, o_ref,
                       pt_ref, qt_ref, g_ref, y0_ref, y_ref, *, n_chunks, n_pairs):
    c2 = 2 * CHUNK
    lane = lax.broadcasted_iota(jnp.int32, (CHUNK, PAIR), 1)
    head0 = lane < RWKV_HEAD
    ri = lax.broadcasted_iota(jnp.int32, (2 * c2, 2 * c2), 0)
    ci = lax.broadcasted_iota(jnp.int32, (2 * c2, 2 * c2), 1)
    keep = (ci & (CHUNK - 1)) < (ri & (CHUNK - 1)) + jnp.where(ri < c2, 0, 1)
    eye = (lax.broadcasted_iota(jnp.int32, (c2, c2), 0)
           == lax.broadcasted_iota(jnp.int32, (c2, c2), 1)).astype(F32)
    zeros_b = jnp.zeros((c2, c2), BF16)

    def stacked(ref, sl, ls):
        x = ref[sl, ls]
        zero = jnp.zeros_like(x)
        return jnp.concatenate([jnp.where(head0, x, zero), jnp.where(head0, zero, x)], axis=0)

    def dot(a, b):
        return jnp.dot(a, b, preferred_element_type=F32)

    def phase1_stages(items):
        def ld(ref, it):
            p, c = it
            return stacked(ref, pl.ds(pl.multiple_of(c * CHUNK, CHUNK), CHUNK),
                           slice(p * PAIR, (p + 1) * PAIR))

        lms = [lax.dot_general(jnp.concatenate([ld(at_ref, it), ld(rt_ref, it)], axis=0),
                               jnp.concatenate([ld(bt_ref, it), ld(kt_ref, it)], axis=0), _NT,
                               preferred_element_type=F32) for it in items]
        lms = [jnp.where(keep, lm, 0.0) for lm in lms]
        tops = [lm[:c2, :].astype(BF16) for lm in lms]
        m_rs = [lm[c2:, :].astype(BF16) for lm in lms]
        yield
        invs = [eye + lm[:c2, :c2] for lm in lms]
        firsts = [dot(top, jnp.concatenate([jnp.concatenate([top[:, :c2], zeros_b], axis=1),
                                            jnp.concatenate([zeros_b, ld(v_ref, it)], axis=1)], axis=0))
                  for top, it in zip(tops, items)]
        lps = [f[:, :c2] for f in firsts]
        lak_vs = [f[:, c2:].astype(BF16) for f in firsts]
        yield
        n_fac = int(np.log2(CHUNK)) - 1
        for f in range(n_fac - 1):
            lp_bs = [lp.astype(BF16) for lp in lps]
            prs = [dot(lp_b, jnp.concatenate([lp_b, inv.astype(BF16)], axis=1))
                   for lp_b, inv in zip(lp_bs, invs)]
            lps = [pr[:, :c2] for pr in prs]
            invs = [inv + pr[:, c2:] for inv, pr in zip(invs, prs)]
            yield
        invs = [inv + dot(lp.astype(BF16), inv.astype(BF16)) for lp, inv in zip(lps, invs)]
        yield
        xu_bs = [dot(inv.astype(BF16), jnp.concatenate([ld(at_ref, it), lak_v], axis=1)).astype(BF16)
                 for inv, lak_v, it in zip(invs, lak_vs, items)]
        rhss = [jnp.concatenate([xu_b, jnp.concatenate([zeros_b, ld(v_ref, it)], axis=1)], axis=0)
                for xu_b, it in zip(xu_bs, items)]
        yield
        gys = [dot(m_r, rhs) for m_r, rhs in zip(m_rs, rhss)]
        yield
        pqs = [lax.dot_general(rhs, jnp.concatenate([ld(bd_ref, it), ld(kd_ref, it)], axis=0), _TN,
                               preferred_element_type=F32)
               for rhs, it in zip(rhss, items)]
        for (p, c), gy, pq in zip(items, gys, pqs):
            g_ref[p, c] = (ld(rt_ref, (p, c)).astype(F32) + gy[:, :c2]).astype(BF16)
            y0_ref[p, c] = gy[:, c2:]
            pt_ref[p, c] = pq[:c2, :].astype(BF16)
            qt_ref[p, c] = pq[c2:, :]

    def phase2_chunk(c, sts):
        sl = pl.ds(pl.multiple_of(c * CHUNK, CHUNK), CHUNK)
        new = []
        for p in range(n_pairs):
            ls = slice(p * PAIR, (p + 1) * PAIR)
            st = sts[p]
            st_b = st.astype(BF16)
            y_d = lax.dot_general(g_ref[p, c], st_b, _NT, preferred_element_type=F32) + y0_ref[p, c]
            y_ref[sl, ls] = y_d[:CHUNK, :] + y_d[CHUNK:, :]
            new.append(st * wc_ref[0, c, :, ls] + dot(st_b, pt_ref[p, c]) + qt_ref[p, c])
        return tuple(new)

    group = 4
    n_groups = n_chunks // group

    def run_group(i, sts, build, scan):
        stages = (phase1_stages([(p, i * group + u) for p in range(n_pairs) for u in range(group)])
                  if build else iter(()))
        todo = [(i - 1) * group + u for u in range(group)] if scan else []
        for s, _ in enumerate(stages):
            if todo and s % 2 == 1:
                sts = phase2_chunk(todo.pop(0), sts)
        for c in todo:
            sts = phase2_chunk(c, sts)
        return sts

    sts = tuple(jnp.zeros((c2, c2), F32) for _ in range(n_pairs))
    sts = run_group(0, sts, True, False)
    sts = lax.fori_loop(1, n_groups, lambda i, s: run_group(i, s, True, True), sts)
    run_group(n_groups, sts, False, True)

    rows3 = 4 * CHUNK
    avg = avg_ref[...]

    unroll3 = 2

    def phase3(i, carry):
        items = [(pl.ds(pl.multiple_of((i * unroll3 + u) * rows3, rows3), rows3),
                  slice(p * PAIR, (p + 1) * PAIR)) for u in range(unroll3) for p in range(n_pairs)]
        ys = [y_ref[sl, ls] for sl, ls in items]
        ds = [y - _dot_exact_rhs(y, avg) for y in ys]
        vs = [_dot_exact_rhs(d * d, avg) for d in ds]
        for (sl, ls), d, var in zip(items, ds, vs):
            yn = d * lax.rsqrt(var + GN_EPS) * gnw_ref[:, ls] + gnb_ref[:, ls]
            o_ref[sl, ls] = ((yn + bonus_ref[sl, ls]) * gate_ref[sl, ls].astype(F32)).astype(o_ref.dtype)
        return carry

    lax.fori_loop(0, n_chunks * CHUNK // (rows3 * unroll3), phase3, 0)


def _rwkv_chunk(prep, gates, gate_col, gn_w, gn_b, batch, seq_len):
    at, rt, bt, kt, bd, kd, v, bonus, wc = prep
    n, w = at.shape
    n_chunks = seq_len // CHUNK
    n_pairs = 2
    bw = n_pairs * PAIR
    wc = wc.reshape(batch, n_chunks, 1, w)
    avg = jnp.asarray(np.kron(np.eye(PAIR // RWKV_HEAD, dtype=np.float32),
                              np.full((RWKV_HEAD, RWKV_HEAD), 1.0 / RWKV_HEAD, np.float32)), BF16)
    tok = pl.BlockSpec((seq_len, bw), lambda b, h: (b, h))
    gate_spec = pl.BlockSpec((seq_len, bw), lambda b, h: (b, gate_col // bw + h))
    vec = pl.BlockSpec((1, bw), lambda b, h: (0, h))
    c2 = 2 * CHUNK
    ops = lambda dt: pltpu.VMEM((n_pairs, n_chunks, c2, c2), dt)
    return pl.pallas_call(
        functools.partial(_rwkv_chunk_kernel, n_chunks=n_chunks, n_pairs=n_pairs),
        out_shape=jax.ShapeDtypeStruct((n, w), BF16),
        grid=(batch, w // bw),
        in_specs=[tok] * 8 + [pl.BlockSpec((1, n_chunks, 1, bw), lambda b, h: (b, 0, 0, h)),
                              gate_spec, vec, vec,
                              pl.BlockSpec((PAIR, PAIR), lambda b, h: (0, 0))],
        out_specs=tok,
        scratch_shapes=[ops(BF16), ops(F32), ops(BF16), ops(F32), pltpu.VMEM((seq_len, bw), F32)],
        compiler_params=_params("parallel", "parallel"),
        name="rwkv_chunk",
    )(at, rt, bt, kt, bd, kd, v, bonus, wc, gates, gn_w.reshape(1, w), gn_b.reshape(1, w), avg)


def _swa_kernel(sink_ref, q_ref, kp_ref, kc_ref, vp_ref, vc_ref, g_ref, o_ref):
    n = pl.program_id(1)
    lane = lax.broadcasted_iota(jnp.int32, (2 * BLOCK, LANES), 1)
    head0 = lane < SWA_HEAD
    head0_q = lax.broadcasted_iota(jnp.int32, (BLOCK, LANES), 1) < SWA_HEAD
    r = lax.broadcasted_iota(jnp.int32, (BLOCK, 2 * BLOCK), 0)
    c = lax.broadcasted_iota(jnp.int32, (BLOCK, 2 * BLOCK), 1)
    valid = (c > r) & (c <= r + BLOCK) & ((c >= BLOCK) | (n > 0))
    pairs_per_kv = SWA_Q_HEADS // SWA_KV_HEADS // 2

    def block_diag(prev, cur):
        x = jnp.concatenate([prev, cur], axis=0)
        zero = jnp.zeros_like(x)
        return jnp.concatenate([jnp.where(head0, x, zero), jnp.where(head0, zero, x)], axis=0)

    for h in range(SWA_KV_HEADS):
        hs = slice(h * LANES, (h + 1) * LANES)
        kblk = block_diag(kp_ref[:, hs], kc_ref[:, hs])
        vblk = block_diag(vp_ref[:, hs], vc_ref[:, hs])
        pairs = [h * pairs_per_kv + j for j in range(pairs_per_kv)]
        cols = [slice(pair * LANES, (pair + 1) * LANES) for pair in pairs]
        ss = [lax.dot_general(q_ref[:, cs_], kblk, _NT, preferred_element_type=F32) for cs_ in cols]
        heads = [(i, e) for i in range(len(pairs)) for e in range(2)]
        tiles = [jnp.where(valid, ss[i][:, e * 2 * BLOCK:(e + 1) * 2 * BLOCK], -jnp.inf)
                 for i, e in heads]
        sinks = [sink_ref[2 * pairs[i] + e] for i, e in heads]
        ms = [jnp.maximum(jnp.max(t, axis=-1, keepdims=True), sk) for t, sk in zip(tiles, sinks)]
        exs = [jnp.exp(t - m) for t, m in zip(tiles, ms)]
        dens = [jnp.sum(ex, axis=-1, keepdims=True) + jnp.exp(sk - m)
                for ex, sk, m in zip(exs, sinks, ms)]
        outs = [jnp.dot(jnp.concatenate([exs[2 * i].astype(BF16), exs[2 * i + 1].astype(BF16)], axis=1),
                        vblk, preferred_element_type=F32) for i in range(len(pairs))]
        for i, cs_ in enumerate(cols):
            rden = jnp.where(head0_q, 1.0 / dens[2 * i], 1.0 / dens[2 * i + 1])
            o_ref[:, cs_] = (outs[i] * rden * g_ref[:, cs_].astype(F32)).astype(o_ref.dtype)


def _swa(plain, q_col, kv, k_col, v_col, gates, gate_col, sinks, batch, seq_len):
    n = plain.shape[0]
    w = SWA_WIDTH
    nb = seq_len // BLOCK
    kvw = 2 * SWA_KV_HEADS * SWA_HEAD
    cur = lambda col, width: (lambda b, i: (b * nb + i, col // width))
    prev = lambda col, width: (lambda b, i: (b * nb + jnp.maximum(i - 1, 0), col // width))
    return pl.pallas_call(
        _swa_kernel,
        out_shape=jax.ShapeDtypeStruct((n, w), BF16),
        grid=(batch, nb),
        in_specs=[pl.BlockSpec(memory_space=pltpu.SMEM),
                  pl.BlockSpec((BLOCK, w), cur(q_col, w)),
                  pl.BlockSpec((BLOCK, kvw), prev(k_col, kvw)), pl.BlockSpec((BLOCK, kvw), cur(k_col, kvw)),
                  pl.BlockSpec((BLOCK, kvw), prev(v_col, kvw)), pl.BlockSpec((BLOCK, kvw), cur(v_col, kvw)),
                  pl.BlockSpec((BLOCK, w), cur(gate_col, w))],
        out_specs=pl.BlockSpec((BLOCK, w), cur(0, w)),
        compiler_params=_params("parallel", "parallel"),
        name="swa",
    )(sinks, plain, kv, kv, kv, kv, gates)


def _xattn_kernel(q_ref, kv_ref, g_ref, o_ref):
    for h in range(XA_HEADS):
        hs = slice(h * XA_HEAD, (h + 1) * XA_HEAD)
        vs = slice(XA_WIDTH + h * XA_HEAD, XA_WIDTH + (h + 1) * XA_HEAD)
        s = lax.dot_general(q_ref[:, hs], kv_ref[:, hs], _NT, preferred_element_type=F32)
        m = jnp.max(s, axis=-1, keepdims=True)
        ex = jnp.exp(s - m)
        p = (ex / jnp.sum(ex, axis=-1, keepdims=True)).astype(BF16)
        o = jnp.dot(p, kv_ref[:, vs], preferred_element_type=F32)
        o_ref[:, hs] = (o * g_ref[:, hs].astype(F32)).astype(o_ref.dtype)


def _xattn(plain, q_col, kv, gates, gate_col, batch, seq_len, tq):
    n = plain.shape[0]
    w = XA_WIDTH
    tq = min(tq, seq_len)
    nq = seq_len // tq
    m = kv.shape[0] // batch
    tok = lambda col: pl.BlockSpec((tq, w), lambda b, i: (b * nq + i, col // w))
    return pl.pallas_call(
        _xattn_kernel,
        out_shape=jax.ShapeDtypeStruct((n, w), BF16),
        grid=(batch, nq),
        in_specs=[tok(q_col), pl.BlockSpec((m, kv.shape[1]), lambda b, i: (b, 0)), tok(gate_col)],
        out_specs=tok(0),
        compiler_params=_params("parallel", "parallel"),
        name="xattn",
    )(plain, kv, gates)


def _merge_kernel(ya_ref, yb_ref, yc_ref, ga_ref, gb_ref, gc_ref, wa_ref, wb_ref, wc_ref, o_ref,
                  wab_ref, wbb_ref, wcb_ref):
    @pl.when(pl.program_id(1) == 0)
    def _():
        wab_ref[...] = wa_ref[0].astype(BF16)
        wbb_ref[...] = wb_ref[0].astype(BF16)
        wcb_ref[...] = wc_ref[0].astype(BF16)

    d = lambda y, w: jnp.dot(y[...], w[...], preferred_element_type=F32)
    acc = ga_ref[...].astype(F32) * d(ya_ref, wab_ref)
    acc += gb_ref[...].astype(F32) * d(yb_ref, wbb_ref)
    acc += gc_ref[...].astype(F32) * d(yc_ref, wcb_ref)
    o_ref[...] = acc.astype(o_ref.dtype)


def _merge(ya, yb, yc, gates, wa, wb, wc, layer, tm, tn):
    n = ya.shape[0]
    d = wa.shape[2]
    tm = min(tm, n)
    nj = d // tn
    y_spec = lambda a: pl.BlockSpec((tm, a.shape[1]), lambda j, i: (i, 0))
    w_spec = lambda a: pl.BlockSpec((1, a.shape[1], tn), lambda j, i: (layer, 0, j),
                                    pipeline_mode=pl.Buffered(1))
    g_spec = lambda br: pl.BlockSpec((tm, tn), lambda j, i: (i, br * nj + j))
    return pl.pallas_call(
        _merge_kernel,
        out_shape=jax.ShapeDtypeStruct((n, d), BF16),
        grid=(nj, n // tm),
        in_specs=[y_spec(ya), y_spec(yb), y_spec(yc), g_spec(0), g_spec(1), g_spec(2),
                  w_spec(wa), w_spec(wb), w_spec(wc)],
        out_specs=pl.BlockSpec((tm, tn), lambda j, i: (i, j)),
        scratch_shapes=[pltpu.VMEM((a.shape[1], tn), BF16) for a in (wa, wb, wc)],
        compiler_params=_params("arbitrary", "arbitrary"),
        name="merge",
    )(ya, yb, yc, gates, gates, gates, wa, wb, wc)


def _out_kernel(m_ref, w_ref, x_ref, g_ref, o_ref, wb_ref):
    @pl.when(pl.program_id(0) == 0)
    def _():
        wb_ref[...] = w_ref[0].astype(BF16)

    o = jnp.dot(m_ref[...], wb_ref[...], preferred_element_type=F32)
    ms = jnp.mean(o * o, axis=-1, keepdims=True)
    o_ref[...] = x_ref[...] + o * lax.rsqrt(ms + NORM_EPS) * g_ref[...]


def _out_proj(merged, w_out, layer, x2d, g_post, tm):
    n, d = x2d.shape
    tm = min(tm, n)
    tok = pl.BlockSpec((tm, d), lambda i: (i, 0))
    return pl.pallas_call(
        _out_kernel,
        out_shape=jax.ShapeDtypeStruct((n, d), F32),
        grid=(n // tm,),
        in_specs=[tok, pl.BlockSpec((1, d, d), lambda i: (layer, 0, 0), pipeline_mode=pl.Buffered(1)),
                  tok, pl.BlockSpec((1, d), lambda i: (0, 0))],
        out_specs=tok,
        scratch_shapes=[pltpu.VMEM((d, d), BF16)],
        compiler_params=_params("arbitrary"),
        name="out_proj",
    )(merged, w_out, x2d, g_post.reshape(1, d))


def kernel(x, mem, g_pre, w_in, mu_shift, decay_base, decay_up, iclr_base, iclr_up, k_k, k_a, r_k,
           gn_w, gn_b, attn_sinks, g_mem, w_mem_kv, w_up_rwkv, w_up_swa, w_up_xattn, w_out, g_post):
    batch, seq_len, d = x.shape
    n = batch * seq_len
    x2d = x.reshape(n, d)
    mem2d = mem.reshape(batch * mem.shape[1], d)
    w = RWKV_WIDTH
    kvh = SWA_KV_HEADS * SWA_HEAD
    o_gate_a = SHIFT_COLS
    o_q_b = o_gate_a + w
    o_kv_b = o_q_b + SWA_WIDTH
    o_gate_b = o_kv_b + 2 * kvh
    o_q_c = o_gate_b + SWA_WIDTH
    o_gate_c = o_q_c + XA_WIDTH
    o_merge = o_gate_c + XA_WIDTH

    tn = 1024
    blocks = lambda off, width: [off + t * tn for t in range(width // tn)]
    for l in range(DEPTH):
        wda = jnp.zeros((LANES, 2 * w), F32)
        wda = wda.at[:DECAY_RANK, :w].set(decay_up[l]).at[DECAY_RANK:, w:].set(iclr_up[l])

        h = _rmsnorm(x2d, g_pre[l], 512)
        p_rkv = _proj(h, w_in, l, blocks(0, 3 * w), tn, act=None, out_dtype=F32, tm=1024, name="proj_rkv")
        p_da = _proj(h, w_in, l, [3 * w], LANES, act=None, out_dtype=F32, tm=1024, name="proj_da")
        p_gates = _proj(h, w_in, l, [o_gate_a, o_gate_b, o_gate_c], tn, act="silu", out_dtype=BF16,
                        tm=1024, name="proj_gates")
        p_q = _proj(h, w_in, l, [o_q_b, o_q_c], tn, act=None, out_dtype=BF16, tm=1024, name="proj_q",
                    scales=[SWA_HEAD ** -0.5, XA_HEAD ** -0.5])
        p_kv = _proj_kv(h, w_in, l, o_kv_b, 1024)
        p_merge = _proj(h, w_in, l, blocks(o_merge, N_BRANCH * d), tn, act="sigmoid", out_dtype=BF16,
                        tm=1024, name="proj_merge")

        prep = _rwkv_prep(p_rkv, p_da, seq_len, mu_shift[l], wda, decay_base[l], iclr_base[l],
                          k_k[l], k_a[l], r_k[l].reshape(-1), 256)
        y_a = _rwkv_chunk(prep, p_gates, 0, gn_w[l], gn_b[l], batch, seq_len)
        y_b = _swa(p_q, 0, p_kv, 0, 2 * kvh, p_gates, w, attn_sinks[l], batch, seq_len)

        mem_n = _rmsnorm(mem2d, g_mem[l], 512)
        kv_c = _proj(mem_n, w_mem_kv, l, blocks(0, 2 * XA_WIDTH), tn, act=None, out_dtype=BF16, tm=1024,
                     name="mem_kv")
        y_c = _xattn(p_q, SWA_WIDTH, kv_c, p_gates, w + SWA_WIDTH, batch, seq_len, 512)

        merged = _merge(y_a, y_b, y_c, p_merge, w_up_rwkv, w_up_swa, w_up_xattn, l, 1024, 1024)
        x2d = _out_proj(merged, w_out, l, x2d, g_post[l], 512)
    return x2d.reshape(batch, seq_len, d)
```

```python
import functools

import jax
import jax.numpy as jnp
import numpy as np
from jax import lax
from jax.experimental import pallas as pl
from jax.experimental.pallas import tpu as pltpu

F32 = jnp.float32
BF16 = jnp.bfloat16

D_MODEL = 2048
DEPTH = 2
MEM_LEN = 256
NORM_EPS = 1e-6
RWKV_WIDTH = 1024
RWKV_HEAD = 64
DECAY_RANK = 64
ICLR_RANK = 64
GN_EPS = 64e-5
SWA_HEAD = 64
SWA_Q_HEADS = 16
SWA_KV_HEADS = 2
SWA_WIDTH = SWA_Q_HEADS * SWA_HEAD
BLOCK = 128
XA_HEADS = 4
XA_HEAD = 256
XA_WIDTH = XA_HEADS * XA_HEAD
N_BRANCH = 3
SHIFT_COLS = 3 * RWKV_WIDTH + DECAY_RANK + ICLR_RANK

LANES = 128
CHUNK = 64
PAIR = 2 * RWKV_HEAD
VMEM_LIMIT = 56 * 1024 * 1024

_NT = (((1,), (1,)), ((), ()))
_TN = (((0,), (0,)), ((), ()))


def _params(*sem):
    return pltpu.CompilerParams(dimension_semantics=sem, vmem_limit_bytes=VMEM_LIMIT)


def _sigmoid(x):
    return 1.0 / (1.0 + jnp.exp(-x))


def _split3(x):
    hi = x.astype(BF16)
    r1 = x - hi.astype(F32)
    mid = r1.astype(BF16)
    lo = (r1 - mid.astype(F32)).astype(BF16)
    return hi, mid, lo


def _dot_exact_rhs(x, m):
    hi, mid, lo = _split3(x)
    d = lambda a: jnp.dot(a, m, preferred_element_type=F32)
    return d(hi) + d(mid) + d(lo)


def _dot_exact_lhs(m, x):
    hi, mid, lo = _split3(x)
    d = lambda a: jnp.dot(m, a, preferred_element_type=F32)
    return d(hi) + d(mid) + d(lo)


def _rmsnorm_kernel(x_ref, g_ref, o_ref):
    xf = x_ref[...]
    ms = jnp.mean(xf * xf, axis=-1, keepdims=True)
    o_ref[...] = (xf * lax.rsqrt(ms + NORM_EPS) * g_ref[...]).astype(o_ref.dtype)


def _rmsnorm(x2d, g, tm):
    m, d = x2d.shape
    tm = min(tm, m)
    return pl.pallas_call(
        _rmsnorm_kernel,
        out_shape=jax.ShapeDtypeStruct((m, d), BF16),
        grid=(m // tm,),
        in_specs=[pl.BlockSpec((tm, d), lambda i: (i, 0)),
                  pl.BlockSpec((1, d), lambda i: (0, 0))],
        out_specs=pl.BlockSpec((tm, d), lambda i: (i, 0)),
        compiler_params=_params("parallel"),
        name="rmsnorm",
    )(x2d, g.reshape(1, d))


def _pick(j, values):
    out = values[0]
    for idx in range(1, len(values)):
        out = jnp.where(j == idx, values[idx], out)
    return out


def _w_spec(layer, k, tn, col_offs):
    assert all(off % LANES == 0 for off in col_offs)
    if len(col_offs) == 1:
        return pl.BlockSpec((pl.Element(1), pl.Element(k), pl.Element(tn)),
                            lambda j, i: (layer, 0, col_offs[0]))
    return pl.BlockSpec((pl.Element(1), pl.Element(k), pl.Element(tn)),
                        lambda j, i: (layer, 0, pl.multiple_of(_pick(j, col_offs), LANES)))


def _proj_kernel(h_ref, w_ref, o_ref, wb_ref, *, act, scales):
    j = pl.program_id(0)

    @pl.when(pl.program_id(1) == 0)
    def _():
        wf = w_ref[0]
        if scales is not None:
            wf = wf * _pick(j, scales)
        wb_ref[...] = wf.astype(BF16)

    acc = jnp.dot(h_ref[...], wb_ref[...], preferred_element_type=F32)
    if act == "silu":
        acc = acc * _sigmoid(acc)
    elif act == "sigmoid":
        acc = _sigmoid(acc)
    o_ref[...] = acc.astype(o_ref.dtype)


def _proj(h, w, layer, col_offs, tn, *, act, out_dtype, tm, name, scales=None):
    m, k = h.shape
    tm = min(tm, m)
    nb = len(col_offs)
    return pl.pallas_call(
        functools.partial(_proj_kernel, act=act, scales=scales),
        out_shape=jax.ShapeDtypeStruct((m, nb * tn), out_dtype),
        grid=(nb, m // tm),
        in_specs=[pl.BlockSpec((tm, k), lambda j, i: (i, 0)), _w_spec(layer, k, tn, col_offs)],
        out_specs=pl.BlockSpec((tm, tn), lambda j, i: (i, j)),
        scratch_shapes=[pltpu.VMEM((k, tn), BF16)],
        compiler_params=_params("arbitrary", "arbitrary"),
        name=name,
    )(h, w)


def _proj_kv_kernel(h_ref, w_ref, o_ref, wb_ref):
    @pl.when(pl.program_id(0) == 0)
    def _():
        wf = w_ref[0]
        head0 = lax.broadcasted_iota(jnp.int32, (wf.shape[0], LANES), 1) < SWA_HEAD
        for g in range(wf.shape[1] // LANES):
            a = wf[:, g * LANES:(g + 1) * LANES]
            ra = pltpu.roll(a, SWA_HEAD, 1)
            wb_ref[:, 2 * g * LANES:(2 * g + 1) * LANES] = jnp.where(head0, a, ra).astype(BF16)
            wb_ref[:, (2 * g + 1) * LANES:(2 * g + 2) * LANES] = jnp.where(head0, ra, a).astype(BF16)

    o_ref[...] = jnp.dot(h_ref[...], wb_ref[...], preferred_element_type=F32).astype(o_ref.dtype)


def _proj_kv(h, w, layer, col_off, tm):
    m, k = h.shape
    tm = min(tm, m)
    wi = 2 * SWA_KV_HEADS * SWA_HEAD
    return pl.pallas_call(
        _proj_kv_kernel,
        out_shape=jax.ShapeDtypeStruct((m, 2 * wi), BF16),
        grid=(m // tm,),
        in_specs=[pl.BlockSpec((tm, k), lambda i: (i, 0)),
                  pl.BlockSpec((pl.Element(1), pl.Element(k), pl.Element(wi)),
                               lambda i: (layer, 0, col_off))],
        out_specs=pl.BlockSpec((tm, 2 * wi), lambda i: (i, 0)),
        scratch_shapes=[pltpu.VMEM((k, 2 * wi), BF16)],
        compiler_params=_params("arbitrary"),
        name="proj_kv",
    )(h, w)


def _rwkv_prep_kernel(p_ref, prev_ref, z_ref, zprev_ref, mu_ref, wst_ref, wlo_ref, dbase_ref, ibase_ref, kk_ref, ka_ref,
                      rk_ref, tri_ref, ones_ref,
                      at_ref, rt_ref, bt_ref, kt_ref, bd_ref, kd_ref, v_ref, bonus_ref, wc_ref,
                      *, blocks_per_seq):
    tp = p_ref.shape[0]
    w = RWKV_WIDTH
    gw = 2 * LANES
    first = (pl.program_id(0) % blocks_per_seq) == 0

    def shifted(c0, width, cur_ref=p_ref, before_ref=prev_ref, mu0=0):
        cols = slice(c0, c0 + width)
        x = cur_ref[:, cols]
        xs = pltpu.roll(x, 1, 0)
        prev_last = jnp.where(first, 0.0, before_ref[7:8, cols])
        row = lax.broadcasted_iota(jnp.int32, (8, width), 0)
        xprev = jnp.concatenate([jnp.where(row == 0, prev_last, xs[:8]), xs[8:]], axis=0)
        return x + mu_ref[:, mu0 + c0:mu0 + c0 + width] * (xprev - x)

    def dot(a, b):
        return jnp.dot(a, b, preferred_element_type=F32)

    def split2(x):
        hi = x.astype(BF16)
        return hi, (x - hi.astype(F32)).astype(BF16)

    def head_sum(t):
        hi, lo = split2(t)
        return dot(hi, ones_ref[...]) + dot(lo, ones_ref[...])

    z = shifted(0, LANES, z_ref, zprev_ref, 3 * w)
    lane = lax.broadcasted_iota(jnp.int32, z.shape, 1)
    z = jnp.where(lane < DECAY_RANK, jnp.tanh(z), z)
    z_hi, z_lo = split2(z)
    zz = jnp.concatenate([z_hi, z_lo], axis=1)

    def up_proj(c0):
        cols = slice(c0, c0 + gw)
        return dot(zz, wst_ref[:, cols]) + dot(z_hi, wlo_ref[:, cols])

    for g in range(w // gw):
        c0 = g * gw
        cols = slice(c0, c0 + gw)
        r = shifted(c0, gw)
        k = shifted(w + c0, gw)
        v = shifted(2 * w + c0, gw)
        nz = -(dbase_ref[:, cols] + up_proj(c0))
        a = _sigmoid(ibase_ref[:, cols] + up_proj(w + c0))
        softplus = jnp.maximum(nz, 0.0) + jnp.log(1.0 + jnp.exp(-jnp.abs(nz)))
        logw = -jnp.exp(-softplus - 0.5)

        cs = _dot_exact_lhs(tri_ref[...], logw)
        lasts = [cs[c * CHUNK + CHUNK - 1:c * CHUNK + CHUNK, :] for c in range(tp // CHUNK)]
        cs_last = jnp.concatenate([jnp.broadcast_to(t, (CHUNK, gw)) for t in lasts], axis=0)

        kk = k * kk_ref[:, cols]
        kk = kk * jnp.minimum(lax.rsqrt(head_sum(kk * kk)), 1e12)
        kp = k * (1.0 + (a - 1.0) * ka_ref[:, cols])
        b = kk * a
        e_neg = jnp.exp(-cs)
        e_d = jnp.exp(cs_last - cs)
        at_ref[:, cols] = (-kk * jnp.exp(cs - logw)).astype(BF16)
        rt_ref[:, cols] = (r * jnp.exp(cs)).astype(BF16)
        bt_ref[:, cols] = (b * e_neg).astype(BF16)
        kt_ref[:, cols] = (kp * e_neg).astype(BF16)
        bd_ref[:, cols] = (b * e_d).astype(BF16)
        kd_ref[:, cols] = (kp * e_d).astype(BF16)
        v_ref[:, cols] = v.astype(BF16)
        bonus_ref[:, cols] = head_sum(r * kp * rk_ref[:, cols]) * v
        for c, t in enumerate(lasts):
            wc_ref[0, c:c + 1, cols] = jnp.exp(t)


def _rwkv_prep(p, pz, seq_len, mu, wda, dbase, ibase, k_k, k_a, r_k, tp):
    n = p.shape[0]
    w = RWKV_WIDTH
    tp = min(tp, seq_len)
    nblk = n // tp
    cpb = tp // CHUNK
    tri = np.tril(np.ones((CHUNK, CHUNK), np.float32))
    tri = jnp.asarray(np.kron(np.eye(cpb, dtype=np.float32), tri), BF16)
    ones_bd = jnp.asarray(np.kron(np.eye(2 * LANES // RWKV_HEAD, dtype=np.float32),
                                  np.ones((RWKV_HEAD, RWKV_HEAD), np.float32)), BF16)
    w_hi = wda.astype(BF16)
    w_lo = (wda - w_hi.astype(F32)).astype(BF16)
    w_st = jnp.concatenate([w_hi, w_hi], axis=0)
    row = lambda a: a.reshape(1, -1)
    full = lambda shape: pl.BlockSpec(shape, lambda i: (0,) * len(shape))
    tok = pl.BlockSpec((tp, w), lambda i: (i, 0))
    before = lambda i: (jnp.maximum(i * (tp // 8) - 1, 0), 0)
    outs = pl.pallas_call(
        functools.partial(_rwkv_prep_kernel, blocks_per_seq=seq_len // tp),
        out_shape=[jax.ShapeDtypeStruct((n, w), BF16)] * 7
        + [jax.ShapeDtypeStruct((n, w), F32), jax.ShapeDtypeStruct((nblk, cpb, w), F32)],
        grid=(nblk,),
        in_specs=[pl.BlockSpec((tp, 3 * w), lambda i: (i, 0)),
                  pl.BlockSpec((8, 3 * w), before),
                  pl.BlockSpec((tp, LANES), lambda i: (i, 0)),
                  pl.BlockSpec((8, LANES), before),
                  full((1, SHIFT_COLS)), full((2 * LANES, 2 * w)), full((LANES, 2 * w)),
                  full((1, w)), full((1, w)), full((1, w)), full((1, w)), full((1, w)),
                  full((tp, tp)), full((2 * LANES, 2 * LANES))],
        out_specs=[tok] * 8 + [pl.BlockSpec((1, cpb, w), lambda i: (i, 0, 0))],
        compiler_params=_params("parallel"),
        name="rwkv_prep",
    )(p, p, pz, pz, row(mu), w_st, w_lo, row(dbase), row(ibase), row(k_k), row(k_a), row(r_k), tri, ones_bd)
    return outs


def _rwkv_chunk_kernel(at_ref, rt_ref, bt_ref, kt_ref, bd_ref, kd_ref, v_ref, bonus_ref, wc_ref,
                       gate_ref, gnw_ref, gnb_ref, avg_ref, o_ref,
                       pt_ref, qt_ref, g_ref, y0_ref, y_ref, *, n_chunks, n_pairs):
    c2 = 2 * CHUNK
    lane = lax.broadcasted_iota(jnp.int32, (CHUNK, PAIR), 1)
    head0 = lane < RWKV_HEAD
    ri = lax.broadcasted_iota(jnp.int32, (2 * c2, 2 * c2), 0)
    ci = lax.broadcasted_iota(jnp.int32, (2 * c2, 2 * c2), 1)
    keep = (ci & (CHUNK - 1)) < (ri & (CHUNK - 1)) + jnp.where(ri < c2, 0, 1)
    eye = (lax.broadcasted_iota(jnp.int32, (c2, c2), 0)
           == lax.broadcasted_iota(jnp.int32, (c2, c2), 1)).astype(F32)
    zeros_b = jnp.zeros((c2, c2), BF16)

    def stacked(ref, sl, ls):
        x = ref[sl, ls]
        zero = jnp.zeros_like(x)
        return jnp.concatenate([jnp.where(head0, x, zero), jnp.where(head0, zero, x)], axis=0)

    def dot(a, b):
        return jnp.dot(a, b, preferred_element_type=F32)

    def phase1_stages(items):
        def ld(ref, it):
            p, c = it
            return stacked(ref, pl.ds(pl.multiple_of(c * CHUNK, CHUNK), CHUNK),
                           slice(p * PAIR, (p + 1) * PAIR))

        lms = [lax.dot_general(jnp.concatenate([ld(at_ref, it), ld(rt_ref, it)], axis=0),
                               jnp.concatenate([ld(bt_ref, it), ld(kt_ref, it)], axis=0), _NT,
                               preferred_element_type=F32) for it in items]
        lms = [jnp.where(keep, lm, 0.0) for lm in lms]
        tops = [lm[:c2, :].astype(BF16) for lm in lms]
        m_rs = [lm[c2:, :].astype(BF16) for lm in lms]
        yield
        invs = [eye + lm[:c2, :c2] for lm in lms]
        firsts = [dot(top, jnp.concatenate([jnp.concatenate([top[:, :c2], zeros_b], axis=1),
                                            jnp.concatenate([zeros_b, ld(v_ref, it)], axis=1)], axis=0))
                  for top, it in zip(tops, items)]
        lps = [f[:, :c2] for f in firsts]
        lak_vs = [f[:, c2:].astype(BF16) for f in firsts]
        yield
        n_fac = int(np.log2(CHUNK)) - 1
        for f in range(n_fac - 1):
            lp_bs = [lp.astype(BF16) for lp in lps]
            prs = [dot(lp_b, jnp.concatenate([lp_b, inv.astype(BF16)], axis=1))
                   for lp_b, inv in zip(lp_bs, invs)]
            lps = [pr[:, :c2] for pr in prs]
            invs = [inv + pr[:, c2:] for inv, pr in zip(invs, prs)]
            yield
        invs = [inv + dot(lp.astype(BF16), inv.astype(BF16)) for lp, inv in zip(lps, invs)]
        yield
        xu_bs = [dot(inv.astype(BF16), jnp.concatenate([ld(at_ref, it), lak_v], axis=1)).astype(BF16)
                 for inv, lak_v, it in zip(invs, lak_vs, items)]
        rhss = [jnp.concatenate([xu_b, jnp.concatenate([zeros_b, ld(v_ref, it)], axis=1)], axis=0)
                for xu_b, it in zip(xu_bs, items)]
        yield
        gys = [dot(m_r, rhs) for m_r, rhs in zip(m_rs, rhss)]
        yield
        pqs = [lax.dot_general(rhs, jnp.concatenate([ld(bd_ref, it), ld(kd_ref, it)], axis=0), _TN,
                               preferred_element_type=F32)
               for rhs, it in zip(rhss, items)]
        for (p, c), gy, pq in zip(items, gys, pqs):
            g_ref[p, c] = (ld(rt_ref, (p, c)).astype(F32) + gy[:, :c2]).astype(BF16)
            y0_ref[p, c] = gy[:, c2:]
            pt_ref[p, c] = pq[:c2, :].astype(BF16)
            qt_ref[p, c] = pq[c2:, :]

    def phase2_chunk(c, sts):
        sl = pl.ds(pl.multiple_of(c * CHUNK, CHUNK), CHUNK)
        new = []
        for p in range(n_pairs):
            ls = slice(p * PAIR, (p + 1) * PAIR)
            st = sts[p]
            st_b = st.astype(BF16)
            y_d = lax.dot_general(g_ref[p, c], st_b, _NT, preferred_element_type=F32) + y0_ref[p, c]
            y_ref[sl, ls] = y_d[:CHUNK, :] + y_d[CHUNK:, :]
            new.append(st * wc_ref[0, c, :, ls] + dot(st_b, pt_ref[p, c]) + qt_ref[p, c])
        return tuple(new)

    group = 4
    n_groups = n_chunks // group

    def run_group(i, sts, build, scan):
        stages = (phase1_stages([(p, i * group + u) for p in range(n_pairs) for u in range(group)])
                  if build else iter(()))
        todo = [(i - 1) * group + u for u in range(group)] if scan else []
        for s, _ in enumerate(stages):
            if todo and s % 2 == 1:
                sts = phase2_chunk(todo.pop(0), sts)
        for c in todo:
            sts = phase2_chunk(c, sts)
        return sts

    sts = tuple(jnp.zeros((c2, c2), F32) for _ in range(n_pairs))
    sts = run_group(0, sts, True, False)
    sts = lax.fori_loop(1, n_groups, lambda i, s: run_group(i, s, True, True), sts)
    run_group(n_groups, sts, False, True)

    rows3 = 4 * CHUNK
    avg = avg_ref[...]

    unroll3 = 2

    def phase3(i, carry):
        items = [(pl.ds(pl.multiple_of((i * unroll3 + u) * rows3, rows3), rows3),
                  slice(p * PAIR, (p + 1) * PAIR)) for u in range(unroll3) for p in range(n_pairs)]
        ys = [y_ref[sl, ls] for sl, ls in items]
        ds = [y - _dot_exact_rhs(y, avg) for y in ys]
        vs = [_dot_exact_rhs(d * d, avg) for d in ds]
        for (sl, ls), d, var in zip(items, ds, vs):
            yn = d * lax.rsqrt(var + GN_EPS) * gnw_ref[:, ls] + gnb_ref[:, ls]
            o_ref[sl, ls] = ((yn + bonus_ref[sl, ls]) * gate_ref[sl, ls].astype(F32)).astype(o_ref.dtype)
        return carry

    lax.fori_loop(0, n_chunks * CHUNK // (rows3 * unroll3), phase3, 0)


def _rwkv_chunk(prep, gates, gate_col, gn_w, gn_b, batch, seq_len):
    at, rt, bt, kt, bd, kd, v, bonus, wc = prep
    n, w = at.shape
    n_chunks = seq_len // CHUNK
    n_pairs = 2
    bw = n_pairs * PAIR
    wc = wc.reshape(batch, n_chunks, 1, w)
    avg = jnp.asarray(np.kron(np.eye(PAIR // RWKV_HEAD, dtype=np.float32),
                              np.full((RWKV_HEAD, RWKV_HEAD), 1.0 / RWKV_HEAD, np.float32)), BF16)
    tok = pl.BlockSpec((seq_len, bw), lambda b, h: (b, h))
    gate_spec = pl.BlockSpec((seq_len, bw), lambda b, h: (b, gate_col // bw + h))
    vec = pl.BlockSpec((1, bw), lambda b, h: (0, h))
    c2 = 2 * CHUNK
    ops = lambda dt: pltpu.VMEM((n_pairs, n_chunks, c2, c2), dt)
    return pl.pallas_call(
        functools.partial(_rwkv_chunk_kernel, n_chunks=n_chunks, n_pairs=n_pairs),
        out_shape=jax.ShapeDtypeStruct((n, w), BF16),
        grid=(batch, w // bw),
        in_specs=[tok] * 8 + [pl.BlockSpec((1, n_chunks, 1, bw), lambda b, h: (b, 0, 0, h)),
                              gate_spec, vec, vec,
                              pl.BlockSpec((PAIR, PAIR), lambda b, h: (0, 0))],
        out_specs=tok,
        scratch_shapes=[ops(BF16), ops(F32), ops(BF16), ops(F32), pltpu.VMEM((seq_len, bw), F32)],
        compiler_params=_params("parallel", "parallel"),
        name="rwkv_chunk",
    )(at, rt, bt, kt, bd, kd, v, bonus, wc, gates, gn_w.reshape(1, w), gn_b.reshape(1, w), avg)


def _swa_stages(n, sink_ref, q_ref, kp_ref, kc_ref, vp_ref, vc_ref, g_ref, o_ref):
    lane = lax.broadcasted_iota(jnp.int32, (2 * BLOCK, LANES), 1)
    head0 = lane < SWA_HEAD
    head0_q = lax.broadcasted_iota(jnp.int32, (BLOCK, LANES), 1) < SWA_HEAD
    r = lax.broadcasted_iota(jnp.int32, (BLOCK, 2 * BLOCK), 0)
    c = lax.broadcasted_iota(jnp.int32, (BLOCK, 2 * BLOCK), 1)
    valid = (c > r) & (c <= r + BLOCK) & ((c >= BLOCK) | (n > 0))
    pairs_per_kv = SWA_Q_HEADS // SWA_KV_HEADS // 2

    def block_diag(prev, cur):
        x = jnp.concatenate([prev, cur], axis=0)
        zero = jnp.zeros_like(x)
        return jnp.concatenate([jnp.where(head0, x, zero), jnp.where(head0, zero, x)], axis=0)

    kv_heads = range(SWA_KV_HEADS)
    lanes = [slice(h * LANES, (h + 1) * LANES) for h in kv_heads]
    kblks = [block_diag(kp_ref[:, hs], kc_ref[:, hs]) for hs in lanes]
    pairs = [[h * pairs_per_kv + j for j in range(pairs_per_kv)] for h in kv_heads]
    cols = [[slice(pair * LANES, (pair + 1) * LANES) for pair in ps_] for ps_ in pairs]
    ss = [[lax.dot_general(q_ref[:, cs_], kblks[h], _NT, preferred_element_type=F32) for cs_ in cols[h]]
          for h in kv_heads]
    yield

    def softmax(h):
        heads = [(i, e) for i in range(pairs_per_kv) for e in range(2)]
        tiles = [jnp.where(valid, ss[h][i][:, e * 2 * BLOCK:(e + 1) * 2 * BLOCK], -jnp.inf)
                 for i, e in heads]
        sinks = [sink_ref[2 * pairs[h][i] + e] for i, e in heads]
        ms = [jnp.maximum(jnp.max(t, axis=-1, keepdims=True), sk) for t, sk in zip(tiles, sinks)]
        exs = [jnp.exp(t - m) for t, m in zip(tiles, ms)]
        dens = [jnp.sum(ex, axis=-1, keepdims=True) + jnp.exp(sk - m)
                for ex, sk, m in zip(exs, sinks, ms)]
        return exs, dens

    def attend(h, exs, dens):
        vblk = block_diag(vp_ref[:, lanes[h]], vc_ref[:, lanes[h]])
        outs = [jnp.dot(jnp.concatenate([exs[2 * i].astype(BF16), exs[2 * i + 1].astype(BF16)], axis=1),
                        vblk, preferred_element_type=F32) for i in range(pairs_per_kv)]
        for i, cs_ in enumerate(cols[h]):
            rden = jnp.where(head0_q, 1.0 / dens[2 * i], 1.0 / dens[2 * i + 1])
            o_ref[:, cs_] = (outs[i] * rden * g_ref[:, cs_].astype(F32)).astype(o_ref.dtype)

    probs = softmax(0)
    for h in kv_heads:
        yield
        attend(h, *probs)
        if h + 1 < SWA_KV_HEADS:
            probs = softmax(h + 1)


def _proj_swa_kernel(h_ref, w_ref, sink_ref, q_ref, kp_ref, kc_ref, vp_ref, vc_ref, g_ref,
                     o_ref, oswa_ref, wb_ref, *, act, n_split, blocks_per_seq):
    @pl.when(pl.program_id(1) == 0)
    def _():
        wb_ref[...] = w_ref[0].astype(BF16)

    step = pl.program_id(0) * pl.num_programs(1) + pl.program_id(1)
    stages = _swa_stages(step % blocks_per_seq, sink_ref, q_ref, kp_ref, kc_ref, vp_ref, vc_ref,
                         g_ref, oswa_ref)
    cw = wb_ref.shape[1] // n_split
    next(stages)
    for s in range(n_split):
        cs_ = slice(s * cw, (s + 1) * cw)
        acc = jnp.dot(h_ref[...], wb_ref[:, cs_], preferred_element_type=F32)
        if act == "sigmoid":
            acc = _sigmoid(acc)
        o_ref[:, cs_] = acc.astype(o_ref.dtype)
        next(stages, None)
    for _ in stages:
        pass


def _proj_swa(h, w, layer, col_offs, tn, act, plain, q_col, kv, k_col, v_col, gates, gate_col, sinks,
              seq_len, tm):
    m, k = h.shape
    tm = min(tm, m)
    ncol, nrow = len(col_offs), m // tm
    assert ncol * nrow * BLOCK == m, "one attention block per projection grid step"
    w_swa = SWA_WIDTH
    nb = seq_len // BLOCK
    kvw = 2 * SWA_KV_HEADS * SWA_HEAD
    step = lambda j, i: j * nrow + i
    cur = lambda col, width: (lambda j, i: (step(j, i), col // width))
    prev = lambda col, width: (
        lambda j, i: (jnp.where(step(j, i) % nb == 0, step(j, i), step(j, i) - 1), col // width))
    return pl.pallas_call(
        functools.partial(_proj_swa_kernel, act=act, n_split=tn // (2 * LANES), blocks_per_seq=nb),
        out_shape=[jax.ShapeDtypeStruct((m, ncol * tn), BF16), jax.ShapeDtypeStruct((m, w_swa), BF16)],
        grid=(ncol, nrow),
        in_specs=[pl.BlockSpec((tm, k), lambda j, i: (i, 0)), _w_spec(layer, k, tn, col_offs),
                  pl.BlockSpec(memory_space=pltpu.SMEM),
                  pl.BlockSpec((BLOCK, w_swa), cur(q_col, w_swa)),
                  pl.BlockSpec((BLOCK, kvw), prev(k_col, kvw)), pl.BlockSpec((BLOCK, kvw), cur(k_col, kvw)),
                  pl.BlockSpec((BLOCK, kvw), prev(v_col, kvw)), pl.BlockSpec((BLOCK, kvw), cur(v_col, kvw)),
                  pl.BlockSpec((BLOCK, w_swa), cur(gate_col, w_swa))],
        out_specs=[pl.BlockSpec((tm, tn), lambda j, i: (i, j)),
                   pl.BlockSpec((BLOCK, w_swa), cur(0, w_swa))],
        scratch_shapes=[pltpu.VMEM((k, tn), BF16)],
        compiler_params=_params("arbitrary", "arbitrary"),
        name="proj_merge_swa",
    )(h, w, sinks, plain, kv, kv, kv, kv, gates)


def _xattn_kernel(q_ref, kv_ref, g_ref, o_ref):
    for h in range(XA_HEADS):
        hs = slice(h * XA_HEAD, (h + 1) * XA_HEAD)
        vs = slice(XA_WIDTH + h * XA_HEAD, XA_WIDTH + (h + 1) * XA_HEAD)
        s = lax.dot_general(q_ref[:, hs], kv_ref[:, hs], _NT, preferred_element_type=F32)
        m = jnp.max(s, axis=-1, keepdims=True)
        ex = jnp.exp(s - m)
        p = (ex / jnp.sum(ex, axis=-1, keepdims=True)).astype(BF16)
        o = jnp.dot(p, kv_ref[:, vs], preferred_element_type=F32)
        o_ref[:, hs] = (o * g_ref[:, hs].astype(F32)).astype(o_ref.dtype)


def _xattn(plain, q_col, kv, gates, gate_col, batch, seq_len, tq):
    n = plain.shape[0]
    w = XA_WIDTH
    tq = min(tq, seq_len)
    nq = seq_len // tq
    m = kv.shape[0] // batch
    tok = lambda col: pl.BlockSpec((tq, w), lambda b, i: (b * nq + i, col // w))
    return pl.pallas_call(
        _xattn_kernel,
        out_shape=jax.ShapeDtypeStruct((n, w), BF16),
        grid=(batch, nq),
        in_specs=[tok(q_col), pl.BlockSpec((m, kv.shape[1]), lambda b, i: (b, 0)), tok(gate_col)],
        out_specs=tok(0),
        compiler_params=_params("parallel", "parallel"),
        name="xattn",
    )(plain, kv, gates)


def _merge_kernel(ya_ref, yb_ref, yc_ref, ga_ref, gb_ref, gc_ref, wa_ref, wb_ref, wc_ref, o_ref,
                  wab_ref, wbb_ref, wcb_ref):
    @pl.when(pl.program_id(1) == 0)
    def _():
        wab_ref[...] = wa_ref[0].astype(BF16)
        wbb_ref[...] = wb_ref[0].astype(BF16)
        wcb_ref[...] = wc_ref[0].astype(BF16)

    d = lambda y, w: jnp.dot(y[...], w[...], preferred_element_type=F32)
    acc = ga_ref[...].astype(F32) * d(ya_ref, wab_ref)
    acc += gb_ref[...].astype(F32) * d(yb_ref, wbb_ref)
    acc += gc_ref[...].astype(F32) * d(yc_ref, wcb_ref)
    o_ref[...] = acc.astype(o_ref.dtype)


def _merge(ya, yb, yc, gates, wa, wb, wc, layer, tm, tn):
    n = ya.shape[0]
    d = wa.shape[2]
    tm = min(tm, n)
    nj = d // tn
    y_spec = lambda a: pl.BlockSpec((tm, a.shape[1]), lambda j, i: (i, 0))
    w_spec = lambda a: pl.BlockSpec((1, a.shape[1], tn), lambda j, i: (layer, 0, j),
                                    pipeline_mode=pl.Buffered(1))
    g_spec = lambda br: pl.BlockSpec((tm, tn), lambda j, i: (i, br * nj + j))
    return pl.pallas_call(
        _merge_kernel,
        out_shape=jax.ShapeDtypeStruct((n, d), BF16),
        grid=(nj, n // tm),
        in_specs=[y_spec(ya), y_spec(yb), y_spec(yc), g_spec(0), g_spec(1), g_spec(2),
                  w_spec(wa), w_spec(wb), w_spec(wc)],
        out_specs=pl.BlockSpec((tm, tn), lambda j, i: (i, j)),
        scratch_shapes=[pltpu.VMEM((a.shape[1], tn), BF16) for a in (wa, wb, wc)],
        compiler_params=_params("arbitrary", "arbitrary"),
        name="merge",
    )(ya, yb, yc, gates, gates, gates, wa, wb, wc)


def _out_kernel(m_ref, w_ref, x_ref, g_ref, *rest):
    gn_ref, o_ref, hn_ref, wb_ref = rest if len(rest) == 4 else (None, rest[0], None, rest[1])

    @pl.when(pl.program_id(0) == 0)
    def _():
        wb_ref[...] = w_ref[0].astype(BF16)

    o = jnp.dot(m_ref[...], wb_ref[...], preferred_element_type=F32)
    ms = jnp.mean(o * o, axis=-1, keepdims=True)
    xn = x_ref[...] + o * lax.rsqrt(ms + NORM_EPS) * g_ref[...]
    o_ref[...] = xn
    if hn_ref is not None:
        ms_n = jnp.mean(xn * xn, axis=-1, keepdims=True)
        hn_ref[...] = (xn * lax.rsqrt(ms_n + NORM_EPS) * gn_ref[...]).astype(hn_ref.dtype)


def _out_proj(merged, w_out, layer, x2d, g_post, g_next, tm):
    n, d = x2d.shape
    tm = min(tm, n)
    tok = pl.BlockSpec((tm, d), lambda i: (i, 0))
    vec = pl.BlockSpec((1, d), lambda i: (0, 0))
    nxt = g_next is not None
    out = pl.pallas_call(
        _out_kernel,
        out_shape=[jax.ShapeDtypeStruct((n, d), F32)] + [jax.ShapeDtypeStruct((n, d), BF16)] * nxt,
        grid=(n // tm,),
        in_specs=[tok, pl.BlockSpec((1, d, d), lambda i: (layer, 0, 0), pipeline_mode=pl.Buffered(1)),
                  tok, vec] + [vec] * nxt,
        out_specs=[tok] + [tok] * nxt,
        scratch_shapes=[pltpu.VMEM((d, d), BF16)],
        compiler_params=_params("arbitrary"),
        name="out_proj",
    )(merged, w_out, x2d, g_post.reshape(1, d), *([g_next.reshape(1, d)] if nxt else []))
    return (out[0], out[1]) if nxt else (out[0], None)


def kernel(x, mem, g_pre, w_in, mu_shift, decay_base, decay_up, iclr_base, iclr_up, k_k, k_a, r_k,
           gn_w, gn_b, attn_sinks, g_mem, w_mem_kv, w_up_rwkv, w_up_swa, w_up_xattn, w_out, g_post):
    batch, seq_len, d = x.shape
    n = batch * seq_len
    x2d = x.reshape(n, d)
    mem2d = mem.reshape(batch * mem.shape[1], d)
    w = RWKV_WIDTH
    kvh = SWA_KV_HEADS * SWA_HEAD
    o_gate_a = SHIFT_COLS
    o_q_b = o_gate_a + w
    o_kv_b = o_q_b + SWA_WIDTH
    o_gate_b = o_kv_b + 2 * kvh
    o_q_c = o_gate_b + SWA_WIDTH
    o_gate_c = o_q_c + XA_WIDTH
    o_merge = o_gate_c + XA_WIDTH

    tn = 1024
    blocks = lambda off, width: [off + t * tn for t in range(width // tn)]
    for l in range(DEPTH):
        wda = jnp.zeros((LANES, 2 * w), F32)
        wda = wda.at[:DECAY_RANK, :w].set(decay_up[l]).at[DECAY_RANK:, w:].set(iclr_up[l])

        if l == 0:
            h = _rmsnorm(x2d, g_pre[l], 512)
        p_rkv = _proj(h, w_in, l, blocks(0, 3 * w), tn, act=None, out_dtype=F32, tm=1024, name="proj_rkv")
        p_da = _proj(h, w_in, l, [3 * w], LANES, act=None, out_dtype=F32, tm=1024, name="proj_da")
        p_gates = _proj(h, w_in, l, [o_gate_a, o_gate_b, o_gate_c], tn, act="silu", out_dtype=BF16,
                        tm=1024, name="proj_gates")
        p_q = _proj(h, w_in, l, [o_q_b, o_q_c], tn, act=None, out_dtype=BF16, tm=1024, name="proj_q",
                    scales=[SWA_HEAD ** -0.5, XA_HEAD ** -0.5])
        p_kv = _proj_kv(h, w_in, l, o_kv_b, 1024)
        tm_m = min(1024, n)
        ncol_m = (n // BLOCK) // (n // tm_m)
        tn_m = N_BRANCH * d // ncol_m
        p_merge, y_b = _proj_swa(h, w_in, l, [o_merge + t * tn_m for t in range(ncol_m)], tn_m, "sigmoid",
                                 p_q, 0, p_kv, 0, 2 * kvh, p_gates, w, attn_sinks[l], seq_len, tm_m)

        prep = _rwkv_prep(p_rkv, p_da, seq_len, mu_shift[l], wda, decay_base[l], iclr_base[l],
                          k_k[l], k_a[l], r_k[l].reshape(-1), 256)
        y_a = _rwkv_chunk(prep, p_gates, 0, gn_w[l], gn_b[l], batch, seq_len)

        mem_n = _rmsnorm(mem2d, g_mem[l], 512)
        kv_c = _proj(mem_n, w_mem_kv, l, blocks(0, 2 * XA_WIDTH), tn, act=None, out_dtype=BF16, tm=1024,
                     name="mem_kv")
        y_c = _xattn(p_q, SWA_WIDTH, kv_c, p_gates, w + SWA_WIDTH, batch, seq_len, 512)

        merged = _merge(y_a, y_b, y_c, p_merge, w_up_rwkv, w_up_swa, w_up_xattn, l, 1024, 1024)
        x2d, h = _out_proj(merged, w_out, l, x2d, g_post[l], g_pre[l + 1] if l + 1 < DEPTH else None, 512)
    return x2d.reshape(batch, seq_len, d)
```

```python
import functools

import jax
import jax.numpy as jnp
import numpy as np
from jax import lax
from jax.experimental import pallas as pl
from jax.experimental.pallas import tpu as pltpu

F32 = jnp.float32
BF16 = jnp.bfloat16

D_MODEL = 2048
DEPTH = 2
MEM_LEN = 256
NORM_EPS = 1e-6
RWKV_WIDTH = 1024
RWKV_HEAD = 64
DECAY_RANK = 64
ICLR_RANK = 64
GN_EPS = 64e-5
SWA_HEAD = 64
SWA_Q_HEADS = 16
SWA_KV_HEADS = 2
SWA_WIDTH = SWA_Q_HEADS * SWA_HEAD
BLOCK = 128
XA_HEADS = 4
XA_HEAD = 256
XA_WIDTH = XA_HEADS * XA_HEAD
N_BRANCH = 3
SHIFT_COLS = 3 * RWKV_WIDTH + DECAY_RANK + ICLR_RANK

LANES = 128
CHUNK = 64
PAIR = 2 * RWKV_HEAD
VMEM_LIMIT = 56 * 1024 * 1024

_NT = (((1,), (1,)), ((), ()))
_TN = (((0,), (0,)), ((), ()))


def _params(*sem):
    return pltpu.CompilerParams(dimension_semantics=sem, vmem_limit_bytes=VMEM_LIMIT)


def _sigmoid(x):
    return 1.0 / (1.0 + jnp.exp(-x))


def _split3(x):
    hi = x.astype(BF16)
    r1 = x - hi.astype(F32)
    mid = r1.astype(BF16)
    lo = (r1 - mid.astype(F32)).astype(BF16)
    return hi, mid, lo


def _dot_exact_lhs(m, x):
    hi, mid, lo = _split3(x)
    d = lambda a: jnp.dot(m, a, preferred_element_type=F32)
    return d(hi) + d(mid) + d(lo)


def _rmsnorm_kernel(x_ref, g_ref, o_ref):
    xf = x_ref[...]
    ms = jnp.mean(xf * xf, axis=-1, keepdims=True)
    o_ref[...] = (xf * lax.rsqrt(ms + NORM_EPS) * g_ref[...]).astype(o_ref.dtype)


def _rmsnorm(x2d, g, tm):
    m, d = x2d.shape
    tm = min(tm, m)
    return pl.pallas_call(
        _rmsnorm_kernel,
        out_shape=jax.ShapeDtypeStruct((m, d), BF16),
        grid=(m // tm,),
        in_specs=[pl.BlockSpec((tm, d), lambda i: (i, 0)),
                  pl.BlockSpec((1, d), lambda i: (0, 0))],
        out_specs=pl.BlockSpec((tm, d), lambda i: (i, 0)),
        compiler_params=_params("parallel"),
        name="rmsnorm",
    )(x2d, g.reshape(1, d))


def _pick(j, values):
    out = values[0]
    for idx in range(1, len(values)):
        out = jnp.where(j == idx, values[idx], out)
    return out


def _w_spec(layer, k, tn, col_offs, single=False):
    assert all(off % LANES == 0 for off in col_offs)
    mode = dict(pipeline_mode=pl.Buffered(1)) if single else {}
    if len(col_offs) == 1:
        return pl.BlockSpec((pl.Element(1), pl.Element(k), pl.Element(tn)),
                            lambda j, i: (layer, 0, col_offs[0]), **mode)
    return pl.BlockSpec((pl.Element(1), pl.Element(k), pl.Element(tn)),
                        lambda j, i: (layer, 0, pl.multiple_of(_pick(j, col_offs), LANES)), **mode)


def _proj_kernel(h_ref, w_ref, o_ref, wb_ref, *, act, scales):
    j = pl.program_id(0)

    @pl.when(pl.program_id(1) == 0)
    def _():
        wf = w_ref[0]
        if scales is not None:
            wf = wf * _pick(j, scales)
        wb_ref[...] = wf.astype(BF16)

    acc = jnp.dot(h_ref[...], wb_ref[...], preferred_element_type=F32)
    if act == "silu":
        acc = acc * _sigmoid(acc)
    elif act == "sigmoid":
        acc = _sigmoid(acc)
    o_ref[...] = acc.astype(o_ref.dtype)


def _proj(h, w, layer, col_offs, tn, *, act, out_dtype, tm, name, scales=None):
    m, k = h.shape
    tm = min(tm, m)
    nb = len(col_offs)
    return pl.pallas_call(
        functools.partial(_proj_kernel, act=act, scales=scales),
        out_shape=jax.ShapeDtypeStruct((m, nb * tn), out_dtype),
        grid=(nb, m // tm),
        in_specs=[pl.BlockSpec((tm, k), lambda j, i: (i, 0)), _w_spec(layer, k, tn, col_offs)],
        out_specs=pl.BlockSpec((tm, tn), lambda j, i: (i, j)),
        scratch_shapes=[pltpu.VMEM((k, tn), BF16)],
        compiler_params=_params("arbitrary", "arbitrary"),
        name=name,
    )(h, w)


def _proj_kv_kernel(h_ref, w_ref, o_ref, wb_ref):
    @pl.when(pl.program_id(0) == 0)
    def _():
        wf = w_ref[0]
        head0 = lax.broadcasted_iota(jnp.int32, (wf.shape[0], LANES), 1) < SWA_HEAD
        for g in range(wf.shape[1] // LANES):
            a = wf[:, g * LANES:(g + 1) * LANES]
            ra = pltpu.roll(a, SWA_HEAD, 1)
            wb_ref[:, 2 * g * LANES:(2 * g + 1) * LANES] = jnp.where(head0, a, ra).astype(BF16)
            wb_ref[:, (2 * g + 1) * LANES:(2 * g + 2) * LANES] = jnp.where(head0, ra, a).astype(BF16)

    o_ref[...] = jnp.dot(h_ref[...], wb_ref[...], preferred_element_type=F32).astype(o_ref.dtype)


def _proj_kv(h, w, layer, col_off, tm):
    m, k = h.shape
    tm = min(tm, m)
    wi = 2 * SWA_KV_HEADS * SWA_HEAD
    return pl.pallas_call(
        _proj_kv_kernel,
        out_shape=jax.ShapeDtypeStruct((m, 2 * wi), BF16),
        grid=(m // tm,),
        in_specs=[pl.BlockSpec((tm, k), lambda i: (i, 0)),
                  pl.BlockSpec((pl.Element(1), pl.Element(k), pl.Element(wi)),
                               lambda i: (layer, 0, col_off))],
        out_specs=pl.BlockSpec((tm, 2 * wi), lambda i: (i, 0)),
        scratch_shapes=[pltpu.VMEM((k, 2 * wi), BF16)],
        compiler_params=_params("arbitrary"),
        name="proj_kv",
    )(h, w)


def _prep_stages(first, p_ref, prev_ref, z_ref, zprev_ref, mu_ref, wst_ref, wlo_ref, dbase_ref, ibase_ref,
                 kk_ref, ka_ref, rk_ref, tri_ref, ones_ref,
                 at_ref, rt_ref, bt_ref, kt_ref, bd_ref, kd_ref, v_ref, bonus_ref, wc_ref):
    tp = p_ref.shape[0]
    w = RWKV_WIDTH
    gw = 2 * LANES
    groups = [slice(g * gw, (g + 1) * gw) for g in range(w // gw)]

    def shifted(c0, width, cur_ref=p_ref, before_ref=prev_ref, mu0=0):
        cols = slice(c0, c0 + width)
        x = cur_ref[:, cols]
        xs = pltpu.roll(x, 1, 0)
        prev_last = jnp.where(first, 0.0, before_ref[7:8, cols])
        row = lax.broadcasted_iota(jnp.int32, (8, width), 0)
        xprev = jnp.concatenate([jnp.where(row == 0, prev_last, xs[:8]), xs[8:]], axis=0)
        return x + mu_ref[:, mu0 + c0:mu0 + c0 + width] * (xprev - x)

    def dot(a, b):
        return jnp.dot(a, b, preferred_element_type=F32)

    def split2(x):
        hi = x.astype(BF16)
        return hi, (x - hi.astype(F32)).astype(BF16)

    def head_sum(t):
        hi, lo = split2(t)
        return dot(hi, ones_ref[...]) + dot(lo, ones_ref[...])

    z = shifted(0, LANES, z_ref, zprev_ref, 3 * w)
    lane = lax.broadcasted_iota(jnp.int32, z.shape, 1)
    z = jnp.where(lane < DECAY_RANK, jnp.tanh(z), z)
    z_hi, z_lo = split2(z)
    zz = jnp.concatenate([z_hi, z_lo], axis=1)
    up = lambda c0: dot(zz, wst_ref[:, c0:c0 + gw]) + dot(z_hi, wlo_ref[:, c0:c0 + gw])
    decs = [up(cols.start) for cols in groups]
    apres = [up(w + cols.start) for cols in groups]
    yield

    a_s, logws = [], []
    for cols, dec, apre in zip(groups, decs, apres):
        nz = -(dbase_ref[:, cols] + dec)
        a_s.append(_sigmoid(ibase_ref[:, cols] + apre))
        softplus = jnp.maximum(nz, 0.0) + jnp.log(1.0 + jnp.exp(-jnp.abs(nz)))
        logws.append(-jnp.exp(-softplus - 0.5))
    css = [_dot_exact_lhs(tri_ref[...], logw) for logw in logws]
    yield

    rs = [shifted(cols.start, gw) for cols in groups]
    ks = [shifted(w + cols.start, gw) for cols in groups]
    vs = [shifted(2 * w + cols.start, gw) for cols in groups]
    kks = [k * kk_ref[:, cols] for k, cols in zip(ks, groups)]
    kps = [k * (1.0 + (a - 1.0) * ka_ref[:, cols]) for k, a, cols in zip(ks, a_s, groups)]
    norms = [head_sum(kk * kk) for kk in kks]
    coefs = [head_sum(r * kp * rk_ref[:, cols]) for r, kp, cols in zip(rs, kps, groups)]
    yield

    for g, cols in enumerate(groups):
        cs, logw, a, kp, v = css[g], logws[g], a_s[g], kps[g], vs[g]
        lasts = [cs[c * CHUNK + CHUNK - 1:c * CHUNK + CHUNK, :] for c in range(tp // CHUNK)]
        cs_last = jnp.concatenate([jnp.broadcast_to(t, (CHUNK, gw)) for t in lasts], axis=0)
        kk = kks[g] * jnp.minimum(lax.rsqrt(norms[g]), 1e12)
        b = kk * a
        e_neg = jnp.exp(-cs)
        e_d = jnp.exp(cs_last - cs)
        at_ref[:, cols] = (-kk * jnp.exp(cs - logw)).astype(BF16)
        rt_ref[:, cols] = (rs[g] * jnp.exp(cs)).astype(BF16)
        bt_ref[:, cols] = (b * e_neg).astype(BF16)
        kt_ref[:, cols] = (kp * e_neg).astype(BF16)
        bd_ref[:, cols] = (b * e_d).astype(BF16)
        kd_ref[:, cols] = (kp * e_d).astype(BF16)
        v_ref[:, cols] = v.astype(BF16)
        bonus_ref[:, cols] = coefs[g] * v
        for c, t in enumerate(lasts):
            wc_ref[0, c:c + 1, cols] = jnp.exp(t)


N_PREP_IN, N_PREP_OUT = 14, 9


def _proj_prep_kernel(h_ref, w_ref, *refs, scales, n_silu, n_split, blocks_per_seq):
    prep_in = refs[:N_PREP_IN]
    o_ref = refs[N_PREP_IN]
    prep_out = refs[N_PREP_IN + 1:N_PREP_IN + 1 + N_PREP_OUT]
    wb_ref = refs[-1]
    j = pl.program_id(0)

    @pl.when(pl.program_id(1) == 0)
    def _():
        wb_ref[...] = (w_ref[0] * _pick(j, scales)).astype(BF16)

    step = j * pl.num_programs(1) + pl.program_id(1)
    stages = _prep_stages(step % blocks_per_seq == 0, *prep_in, *prep_out)
    cw = wb_ref.shape[1] // n_split
    next(stages)
    for s in range(n_split):
        cs_ = slice(s * cw, (s + 1) * cw)
        acc = jnp.dot(h_ref[...], wb_ref[:, cs_], preferred_element_type=F32)
        o_ref[:, cs_] = jnp.where(j < n_silu, acc * _sigmoid(acc), acc).astype(o_ref.dtype)
        next(stages, None)
    for _ in stages:
        pass


def _proj_prep(h, w_in, layer, col_offs, tn, scales, n_silu, p, pz, seq_len, mu, wda, dbase, ibase,
               k_k, k_a, r_k, tm):
    n, k = h.shape
    w = RWKV_WIDTH
    tm = min(tm, n)
    ncol, nrow = len(col_offs), n // tm
    tp = n // (ncol * nrow)
    assert tp % CHUNK == 0 and seq_len % tp == 0 and tp % 8 == 0
    cpb = tp // CHUNK
    tri = np.tril(np.ones((CHUNK, CHUNK), np.float32))
    tri = jnp.asarray(np.kron(np.eye(cpb, dtype=np.float32), tri), BF16)
    ones_bd = jnp.asarray(np.kron(np.eye(2 * LANES // RWKV_HEAD, dtype=np.float32),
                                  np.ones((RWKV_HEAD, RWKV_HEAD), np.float32)), BF16)
    w_hi = wda.astype(BF16)
    w_lo = (wda - w_hi.astype(F32)).astype(BF16)
    w_st = jnp.concatenate([w_hi, w_hi], axis=0)
    row = lambda a: a.reshape(1, -1)
    full = lambda shape: pl.BlockSpec(shape, lambda j, i: (0,) * len(shape))
    step = lambda j, i: j * nrow + i
    tok = lambda width: pl.BlockSpec((tp, width), lambda j, i: (step(j, i), 0))
    before = lambda width: pl.BlockSpec(
        (8, width), lambda j, i: (jnp.maximum(step(j, i) * (tp // 8) - 1, 0), 0))
    outs = pl.pallas_call(
        functools.partial(_proj_prep_kernel, scales=scales, n_silu=n_silu, n_split=tn // (2 * LANES),
                          blocks_per_seq=seq_len // tp),
        out_shape=[jax.ShapeDtypeStruct((n, ncol * tn), BF16)]
        + [jax.ShapeDtypeStruct((n, w), BF16)] * 7
        + [jax.ShapeDtypeStruct((n, w), F32), jax.ShapeDtypeStruct((n // tp, cpb, w), F32)],
        grid=(ncol, nrow),
        in_specs=[pl.BlockSpec((tm, k), lambda j, i: (i, 0)), _w_spec(layer, k, tn, col_offs, single=True),
                  tok(3 * w), before(3 * w), tok(LANES), before(LANES),
                  full((1, SHIFT_COLS)), full((2 * LANES, 2 * w)), full((LANES, 2 * w)),
                  full((1, w)), full((1, w)), full((1, w)), full((1, w)), full((1, w)),
                  full((tp, tp)), full((2 * LANES, 2 * LANES))],
        out_specs=[pl.BlockSpec((tm, tn), lambda j, i: (i, j))] + [tok(w)] * 8
        + [pl.BlockSpec((1, cpb, w), lambda j, i: (step(j, i), 0, 0))],
        scratch_shapes=[pltpu.VMEM((k, tn), BF16)],
        compiler_params=_params("arbitrary", "arbitrary"),
        name="proj_gq_prep",
    )(h, w_in, p, p, pz, pz, row(mu), w_st, w_lo, row(dbase), row(ibase), row(k_k), row(k_a), row(r_k),
      tri, ones_bd)
    return outs[0], outs[1:]


def _rwkv_chunk_kernel(at_ref, rt_ref, bt_ref, kt_ref, bd_ref, kd_ref, v_ref, bonus_ref, wc_ref,
                       gate_ref, gnw_ref, gnb_ref, avg_ref, o_ref,
                       pt_ref, qt_ref, g_ref, y0_ref, y_ref, *, n_chunks, n_pairs):
    c2 = 2 * CHUNK
    lane = lax.broadcasted_iota(jnp.int32, (CHUNK, PAIR), 1)
    head0 = lane < RWKV_HEAD
    ri = lax.broadcasted_iota(jnp.int32, (2 * c2, 2 * c2), 0)
    ci = lax.broadcasted_iota(jnp.int32, (2 * c2, 2 * c2), 1)
    keep = (ci & (CHUNK - 1)) < (ri & (CHUNK - 1)) + jnp.where(ri < c2, 0, 1)
    eye = (lax.broadcasted_iota(jnp.int32, (c2, c2), 0)
           == lax.broadcasted_iota(jnp.int32, (c2, c2), 1)).astype(F32)
    zeros_b = jnp.zeros((c2, c2), BF16)

    def stacked(ref, sl, ls):
        x = ref[sl, ls]
        zero = jnp.zeros_like(x)
        return jnp.concatenate([jnp.where(head0, x, zero), jnp.where(head0, zero, x)], axis=0)

    def dot(a, b):
        return jnp.dot(a, b, preferred_element_type=F32)

    def phase1_stages(items):
        def ld(ref, it):
            p, c = it
            return stacked(ref, pl.ds(pl.multiple_of(c * CHUNK, CHUNK), CHUNK),
                           slice(p * PAIR, (p + 1) * PAIR))

        lms = [lax.dot_general(jnp.concatenate([ld(at_ref, it), ld(rt_ref, it)], axis=0),
                               jnp.concatenate([ld(bt_ref, it), ld(kt_ref, it)], axis=0), _NT,
                               preferred_element_type=F32) for it in items]
        lms = [jnp.where(keep, lm, 0.0) for lm in lms]
        tops = [lm[:c2, :].astype(BF16) for lm in lms]
        m_rs = [lm[c2:, :].astype(BF16) for lm in lms]
        yield
        invs = [eye + lm[:c2, :c2] for lm in lms]
        firsts = [dot(top, jnp.concatenate([jnp.concatenate([top[:, :c2], zeros_b], axis=1),
                                            jnp.concatenate([zeros_b, ld(v_ref, it)], axis=1)], axis=0))
                  for top, it in zip(tops, items)]
        lps = [f[:, :c2] for f in firsts]
        lak_vs = [f[:, c2:].astype(BF16) for f in firsts]
        yield
        n_fac = int(np.log2(CHUNK)) - 1
        for f in range(n_fac - 1):
            lp_bs = [lp.astype(BF16) for lp in lps]
            prs = [dot(lp_b, jnp.concatenate([lp_b, inv.astype(BF16)], axis=1))
                   for lp_b, inv in zip(lp_bs, invs)]
            lps = [pr[:, :c2] for pr in prs]
            invs = [inv + pr[:, c2:] for inv, pr in zip(invs, prs)]
            yield
        invs = [inv + dot(lp.astype(BF16), inv.astype(BF16)) for lp, inv in zip(lps, invs)]
        yield
        xu_bs = [dot(inv.astype(BF16), jnp.concatenate([ld(at_ref, it), lak_v], axis=1)).astype(BF16)
                 for inv, lak_v, it in zip(invs, lak_vs, items)]
        rhss = [jnp.concatenate([xu_b, jnp.concatenate([zeros_b, ld(v_ref, it)], axis=1)], axis=0)
                for xu_b, it in zip(xu_bs, items)]
        yield
        gys = [dot(m_r, rhs) for m_r, rhs in zip(m_rs, rhss)]
        yield
        pqs = [lax.dot_general(rhs, jnp.concatenate([ld(bd_ref, it), ld(kd_ref, it)], axis=0), _TN,
                               preferred_element_type=F32)
               for rhs, it in zip(rhss, items)]
        for (p, c), gy, pq in zip(items, gys, pqs):
            g_ref[p, c] = (ld(rt_ref, (p, c)).astype(F32) + gy[:, :c2]).astype(BF16)
            y0_ref[p, c] = gy[:, c2:]
            pt_ref[p, c] = pq[:c2, :].astype(BF16)
            qt_ref[p, c] = pq[c2:, :]

    def phase2_chunk(c, sts):
        sl = pl.ds(pl.multiple_of(c * CHUNK, CHUNK), CHUNK)
        new = []
        for p in range(n_pairs):
            ls = slice(p * PAIR, (p + 1) * PAIR)
            st = sts[p]
            st_b = st.astype(BF16)
            y_d = lax.dot_general(g_ref[p, c], st_b, _NT, preferred_element_type=F32) + y0_ref[p, c]
            y_ref[sl, ls] = y_d[:CHUNK, :] + y_d[CHUNK:, :]
            new.append(st * wc_ref[0, c, :, ls] + dot(st_b, pt_ref[p, c]) + qt_ref[p, c])
        return tuple(new)

    group = 4
    n_groups = n_chunks // group

    def run_group(i, sts, build, scan):
        stages = (phase1_stages([(p, i * group + u) for p in range(n_pairs) for u in range(group)])
                  if build else iter(()))
        todo = [(i - 1) * group + u for u in range(group)] if scan else []
        for s, _ in enumerate(stages):
            if todo and s % 2 == 1:
                sts = phase2_chunk(todo.pop(0), sts)
        for c in todo:
            sts = phase2_chunk(c, sts)
        return sts

    sts = tuple(jnp.zeros((c2, c2), F32) for _ in range(n_pairs))
    sts = run_group(0, sts, True, False)
    sts = lax.fori_loop(1, n_groups, lambda i, s: run_group(i, s, True, True), sts)
    run_group(n_groups, sts, False, True)

    rows3 = 4 * CHUNK
    head0_r = lax.broadcasted_iota(jnp.int32, (rows3, PAIR), 1) < RWKV_HEAD

    def head_mean(t):
        s0 = jnp.sum(jnp.where(head0_r, t, 0.0), axis=-1, keepdims=True)
        s1 = jnp.sum(jnp.where(head0_r, 0.0, t), axis=-1, keepdims=True)
        return jnp.where(head0_r, s0, s1) * (1.0 / RWKV_HEAD)

    def head_mean_mxu(t):
        hi = t.astype(BF16)
        lo = (t - hi.astype(F32)).astype(BF16)
        return dot(hi, avg_ref[...]) + dot(lo, avg_ref[...])

    unroll3 = 2

    def phase3(i, carry):
        items = [(pl.ds(pl.multiple_of((i * unroll3 + u) * rows3, rows3), rows3),
                  slice(p * PAIR, (p + 1) * PAIR)) for u in range(unroll3) for p in range(n_pairs)]
        ys = [y_ref[sl, ls] for sl, ls in items]
        ds = [y - head_mean(y) for y in ys]
        vs = [head_mean_mxu(d * d) for d in ds]
        for (sl, ls), d, var in zip(items, ds, vs):
            yn = d * lax.rsqrt(var + GN_EPS) * gnw_ref[:, ls] + gnb_ref[:, ls]
            o_ref[sl, ls] = ((yn + bonus_ref[sl, ls]) * gate_ref[sl, ls].astype(F32)).astype(o_ref.dtype)
        return carry

    lax.fori_loop(0, n_chunks * CHUNK // (rows3 * unroll3), phase3, 0)


def _rwkv_chunk(prep, gates, gate_col, gn_w, gn_b, batch, seq_len):
    at, rt, bt, kt, bd, kd, v, bonus, wc = prep
    n, w = at.shape
    n_chunks = seq_len // CHUNK
    n_pairs = 2
    bw = n_pairs * PAIR
    wc = wc.reshape(batch, n_chunks, 1, w)
    avg = jnp.asarray(np.kron(np.eye(PAIR // RWKV_HEAD, dtype=np.float32),
                              np.full((RWKV_HEAD, RWKV_HEAD), 1.0 / RWKV_HEAD, np.float32)), BF16)
    tok = pl.BlockSpec((seq_len, bw), lambda b, h: (b, h))
    gate_spec = pl.BlockSpec((seq_len, bw), lambda b, h: (b, gate_col // bw + h))
    vec = pl.BlockSpec((1, bw), lambda b, h: (0, h))
    c2 = 2 * CHUNK
    ops = lambda dt: pltpu.VMEM((n_pairs, n_chunks, c2, c2), dt)
    return pl.pallas_call(
        functools.partial(_rwkv_chunk_kernel, n_chunks=n_chunks, n_pairs=n_pairs),
        out_shape=jax.ShapeDtypeStruct((n, w), BF16),
        grid=(batch, w // bw),
        in_specs=[tok] * 8 + [pl.BlockSpec((1, n_chunks, 1, bw), lambda b, h: (b, 0, 0, h)),
                              gate_spec, vec, vec,
                              pl.BlockSpec((PAIR, PAIR), lambda b, h: (0, 0))],
        out_specs=tok,
        scratch_shapes=[ops(BF16), ops(F32), ops(BF16), ops(F32), pltpu.VMEM((seq_len, bw), F32)],
        compiler_params=_params("parallel", "parallel"),
        name="rwkv_chunk",
    )(at, rt, bt, kt, bd, kd, v, bonus, wc, gates, gn_w.reshape(1, w), gn_b.reshape(1, w), avg)


def _swa_stages(n, sink_ref, q_ref, kp_ref, kc_ref, vp_ref, vc_ref, g_ref, o_ref):
    lane = lax.broadcasted_iota(jnp.int32, (2 * BLOCK, LANES), 1)
    head0 = lane < SWA_HEAD
    head0_q = lax.broadcasted_iota(jnp.int32, (BLOCK, LANES), 1) < SWA_HEAD
    r = lax.broadcasted_iota(jnp.int32, (BLOCK, 2 * BLOCK), 0)
    c = lax.broadcasted_iota(jnp.int32, (BLOCK, 2 * BLOCK), 1)
    valid = (c > r) & (c <= r + BLOCK) & ((c >= BLOCK) | (n > 0))
    pairs_per_kv = SWA_Q_HEADS // SWA_KV_HEADS // 2

    def block_diag(prev, cur):
        x = jnp.concatenate([prev, cur], axis=0)
        zero = jnp.zeros_like(x)
        return jnp.concatenate([jnp.where(head0, x, zero), jnp.where(head0, zero, x)], axis=0)

    kv_heads = range(SWA_KV_HEADS)
    lanes = [slice(h * LANES, (h + 1) * LANES) for h in kv_heads]
    kblks = [block_diag(kp_ref[:, hs], kc_ref[:, hs]) for hs in lanes]
    pairs = [[h * pairs_per_kv + j for j in range(pairs_per_kv)] for h in kv_heads]
    cols = [[slice(pair * LANES, (pair + 1) * LANES) for pair in ps_] for ps_ in pairs]
    ss = [[lax.dot_general(q_ref[:, cs_], kblks[h], _NT, preferred_element_type=F32) for cs_ in cols[h]]
          for h in kv_heads]
    yield

    def softmax(h):
        heads = [(i, e) for i in range(pairs_per_kv) for e in range(2)]
        tiles = [jnp.where(valid, ss[h][i][:, e * 2 * BLOCK:(e + 1) * 2 * BLOCK], -jnp.inf)
                 for i, e in heads]
        sinks = [sink_ref[2 * pairs[h][i] + e] for i, e in heads]
        ms = [jnp.maximum(jnp.max(t, axis=-1, keepdims=True), sk) for t, sk in zip(tiles, sinks)]
        exs = [jnp.exp(t - m) for t, m in zip(tiles, ms)]
        dens = [jnp.sum(ex, axis=-1, keepdims=True) + jnp.exp(sk - m)
                for ex, sk, m in zip(exs, sinks, ms)]
        return exs, dens

    def attend(h, exs, dens):
        vblk = block_diag(vp_ref[:, lanes[h]], vc_ref[:, lanes[h]])
        outs = [jnp.dot(jnp.concatenate([exs[2 * i].astype(BF16), exs[2 * i + 1].astype(BF16)], axis=1),
                        vblk, preferred_element_type=F32) for i in range(pairs_per_kv)]
        for i, cs_ in enumerate(cols[h]):
            rden = jnp.where(head0_q, 1.0 / dens[2 * i], 1.0 / dens[2 * i + 1])
            o_ref[:, cs_] = (outs[i] * rden * g_ref[:, cs_].astype(F32)).astype(o_ref.dtype)

    probs = softmax(0)
    for h in kv_heads:
        yield
        attend(h, *probs)
        if h + 1 < SWA_KV_HEADS:
            probs = softmax(h + 1)


def _proj_swa_kernel(h_ref, w_ref, sink_ref, q_ref, kp_ref, kc_ref, vp_ref, vc_ref, g_ref,
                     o_ref, oswa_ref, wb_ref, *, act, n_split, blocks_per_seq):
    @pl.when(pl.program_id(1) == 0)
    def _():
        wb_ref[...] = w_ref[0].astype(BF16)

    step = pl.program_id(0) * pl.num_programs(1) + pl.program_id(1)
    stages = _swa_stages(step % blocks_per_seq, sink_ref, q_ref, kp_ref, kc_ref, vp_ref, vc_ref,
                         g_ref, oswa_ref)
    cw = wb_ref.shape[1] // n_split
    next(stages)
    for s in range(n_split):
        cs_ = slice(s * cw, (s + 1) * cw)
        acc = jnp.dot(h_ref[...], wb_ref[:, cs_], preferred_element_type=F32)
        if act == "sigmoid":
            acc = _sigmoid(acc)
        o_ref[:, cs_] = acc.astype(o_ref.dtype)
        next(stages, None)
    for _ in stages:
        pass


def _proj_swa(h, w, layer, col_offs, tn, act, plain, q_col, kv, k_col, v_col, gates, gate_col, sinks,
              seq_len, tm):
    m, k = h.shape
    tm = min(tm, m)
    ncol, nrow = len(col_offs), m // tm
    assert ncol * nrow * BLOCK == m, "one attention block per projection grid step"
    w_swa = SWA_WIDTH
    nb = seq_len // BLOCK
    kvw = 2 * SWA_KV_HEADS * SWA_HEAD
    step = lambda j, i: j * nrow + i
    cur = lambda col, width: (lambda j, i: (step(j, i), col // width))
    prev = lambda col, width: (
        lambda j, i: (jnp.where(step(j, i) % nb == 0, step(j, i), step(j, i) - 1), col // width))
    return pl.pallas_call(
        functools.partial(_proj_swa_kernel, act=act, n_split=tn // (2 * LANES), blocks_per_seq=nb),
        out_shape=[jax.ShapeDtypeStruct((m, ncol * tn), BF16), jax.ShapeDtypeStruct((m, w_swa), BF16)],
        grid=(ncol, nrow),
        in_specs=[pl.BlockSpec((tm, k), lambda j, i: (i, 0)), _w_spec(layer, k, tn, col_offs),
                  pl.BlockSpec(memory_space=pltpu.SMEM),
                  pl.BlockSpec((BLOCK, w_swa), cur(q_col, w_swa)),
                  pl.BlockSpec((BLOCK, kvw), prev(k_col, kvw)), pl.BlockSpec((BLOCK, kvw), cur(k_col, kvw)),
                  pl.BlockSpec((BLOCK, kvw), prev(v_col, kvw)), pl.BlockSpec((BLOCK, kvw), cur(v_col, kvw)),
                  pl.BlockSpec((BLOCK, w_swa), cur(gate_col, w_swa))],
        out_specs=[pl.BlockSpec((tm, tn), lambda j, i: (i, j)),
                   pl.BlockSpec((BLOCK, w_swa), cur(0, w_swa))],
        scratch_shapes=[pltpu.VMEM((k, tn), BF16)],
        compiler_params=_params("arbitrary", "arbitrary"),
        name="proj_merge_swa",
    )(h, w, sinks, plain, kv, kv, kv, kv, gates)


def _xattn_kernel(q_ref, kv_ref, g_ref, o_ref):
    for h in range(XA_HEADS):
        hs = slice(h * XA_HEAD, (h + 1) * XA_HEAD)
        vs = slice(XA_WIDTH + h * XA_HEAD, XA_WIDTH + (h + 1) * XA_HEAD)
        s = lax.dot_general(q_ref[:, hs], kv_ref[:, hs], _NT, preferred_element_type=F32)
        m = jnp.max(s, axis=-1, keepdims=True)
        ex = jnp.exp(s - m)
        p = (ex / jnp.sum(ex, axis=-1, keepdims=True)).astype(BF16)
        o = jnp.dot(p, kv_ref[:, vs], preferred_element_type=F32)
        o_ref[:, hs] = (o * g_ref[:, hs].astype(F32)).astype(o_ref.dtype)


def _xattn(plain, q_col, kv, gates, gate_col, batch, seq_len, tq):
    n = plain.shape[0]
    w = XA_WIDTH
    tq = min(tq, seq_len)
    nq = seq_len // tq
    m = kv.shape[0] // batch
    tok = lambda col: pl.BlockSpec((tq, w), lambda b, i: (b * nq + i, col // w))
    return pl.pallas_call(
        _xattn_kernel,
        out_shape=jax.ShapeDtypeStruct((n, w), BF16),
        grid=(batch, nq),
        in_specs=[tok(q_col), pl.BlockSpec((m, kv.shape[1]), lambda b, i: (b, 0)), tok(gate_col)],
        out_specs=tok(0),
        compiler_params=_params("parallel", "parallel"),
        name="xattn",
    )(plain, kv, gates)


def _merge_kernel(ya_ref, yb_ref, yc_ref, ga_ref, gb_ref, gc_ref, wa_ref, wb_ref, wc_ref, o_ref,
                  wab_ref, wbb_ref, wcb_ref):
    @pl.when(pl.program_id(1) == 0)
    def _():
        wab_ref[...] = wa_ref[0].astype(BF16)
        wbb_ref[...] = wb_ref[0].astype(BF16)
        wcb_ref[...] = wc_ref[0].astype(BF16)

    d = lambda y, w: jnp.dot(y[...], w[...], preferred_element_type=F32)
    acc = ga_ref[...].astype(F32) * d(ya_ref, wab_ref)
    acc += gb_ref[...].astype(F32) * d(yb_ref, wbb_ref)
    acc += gc_ref[...].astype(F32) * d(yc_ref, wcb_ref)
    o_ref[...] = acc.astype(o_ref.dtype)


def _merge(ya, yb, yc, gates, wa, wb, wc, layer, tm, tn):
    n = ya.shape[0]
    d = wa.shape[2]
    tm = min(tm, n)
    nj = d // tn
    y_spec = lambda a: pl.BlockSpec((tm, a.shape[1]), lambda j, i: (i, 0))
    w_spec = lambda a: pl.BlockSpec((1, a.shape[1], tn), lambda j, i: (layer, 0, j),
                                    pipeline_mode=pl.Buffered(1))
    g_spec = lambda br: pl.BlockSpec((tm, tn), lambda j, i: (i, br * nj + j))
    return pl.pallas_call(
        _merge_kernel,
        out_shape=jax.ShapeDtypeStruct((n, d), BF16),
        grid=(nj, n // tm),
        in_specs=[y_spec(ya), y_spec(yb), y_spec(yc), g_spec(0), g_spec(1), g_spec(2),
                  w_spec(wa), w_spec(wb), w_spec(wc)],
        out_specs=pl.BlockSpec((tm, tn), lambda j, i: (i, j)),
        scratch_shapes=[pltpu.VMEM((a.shape[1], tn), BF16) for a in (wa, wb, wc)],
        compiler_params=_params("arbitrary", "arbitrary"),
        name="merge",
    )(ya, yb, yc, gates, gates, gates, wa, wb, wc)


def _out_kernel(m_ref, w_ref, x_ref, g_ref, *rest):
    gn_ref, o_ref, hn_ref, wb_ref = rest if len(rest) == 4 else (None, rest[0], None, rest[1])

    @pl.when(pl.program_id(0) == 0)
    def _():
        wb_ref[...] = w_ref[0].astype(BF16)

    o = jnp.dot(m_ref[...], wb_ref[...], preferred_element_type=F32)
    ms = jnp.mean(o * o, axis=-1, keepdims=True)
    xn = x_ref[...] + o * lax.rsqrt(ms + NORM_EPS) * g_ref[...]
    o_ref[...] = xn
    if hn_ref is not None:
        ms_n = jnp.mean(xn * xn, axis=-1, keepdims=True)
        hn_ref[...] = (xn * lax.rsqrt(ms_n + NORM_EPS) * gn_ref[...]).astype(hn_ref.dtype)


def _out_proj(merged, w_out, layer, x2d, g_post, g_next, tm):
    n, d = x2d.shape
    tm = min(tm, n)
    tok = pl.BlockSpec((tm, d), lambda i: (i, 0))
    vec = pl.BlockSpec((1, d), lambda i: (0, 0))
    nxt = g_next is not None
    out = pl.pallas_call(
        _out_kernel,
        out_shape=[jax.ShapeDtypeStruct((n, d), F32)] + [jax.ShapeDtypeStruct((n, d), BF16)] * nxt,
        grid=(n // tm,),
        in_specs=[tok, pl.BlockSpec((1, d, d), lambda i: (layer, 0, 0), pipeline_mode=pl.Buffered(1)),
                  tok, vec] + [vec] * nxt,
        out_specs=[tok] + [tok] * nxt,
        scratch_shapes=[pltpu.VMEM((d, d), BF16)],
        compiler_params=_params("arbitrary"),
        name="out_proj",
    )(merged, w_out, x2d, g_post.reshape(1, d), *([g_next.reshape(1, d)] if nxt else []))
    return (out[0], out[1]) if nxt else (out[0], None)


def kernel(x, mem, g_pre, w_in, mu_shift, decay_base, decay_up, iclr_base, iclr_up, k_k, k_a, r_k,
           gn_w, gn_b, attn_sinks, g_mem, w_mem_kv, w_up_rwkv, w_up_swa, w_up_xattn, w_out, g_post):
    batch, seq_len, d = x.shape
    n = batch * seq_len
    x2d = x.reshape(n, d)
    mem2d = mem.reshape(batch * mem.shape[1], d)
    w = RWKV_WIDTH
    kvh = SWA_KV_HEADS * SWA_HEAD
    o_gate_a = SHIFT_COLS
    o_q_b = o_gate_a + w
    o_kv_b = o_q_b + SWA_WIDTH
    o_gate_b = o_kv_b + 2 * kvh
    o_q_c = o_gate_b + SWA_WIDTH
    o_gate_c = o_q_c + XA_WIDTH
    o_merge = o_gate_c + XA_WIDTH

    tn = 1024
    blocks = lambda off, width: [off + t * tn for t in range(width // tn)]
    for l in range(DEPTH):
        wda = jnp.zeros((LANES, 2 * w), F32)
        wda = wda.at[:DECAY_RANK, :w].set(decay_up[l]).at[DECAY_RANK:, w:].set(iclr_up[l])

        if l == 0:
            h = _rmsnorm(x2d, g_pre[l], 512)
        p_rkv = _proj(h, w_in, l, blocks(0, 3 * w), tn, act=None, out_dtype=F32, tm=1024, name="proj_rkv")
        p_da = _proj(h, w_in, l, [3 * w], LANES, act=None, out_dtype=F32, tm=1024, name="proj_da")
        p_gq, prep = _proj_prep(h, w_in, l, [o_gate_a, o_gate_b, o_gate_c, o_q_b], tn,
                                [1.0, 1.0, 1.0, SWA_HEAD ** -0.5], 3, p_rkv, p_da, seq_len, mu_shift[l], wda,
                                decay_base[l], iclr_base[l], k_k[l], k_a[l], r_k[l].reshape(-1), 1024)
        p_qc = _proj(h, w_in, l, [o_q_c], tn, act=None, out_dtype=BF16, tm=1024, name="proj_qc",
                     scales=[XA_HEAD ** -0.5])
        p_kv = _proj_kv(h, w_in, l, o_kv_b, 1024)
        tm_m = min(1024, n)
        ncol_m = (n // BLOCK) // (n // tm_m)
        tn_m = N_BRANCH * d // ncol_m
        p_merge, y_b = _proj_swa(h, w_in, l, [o_merge + t * tn_m for t in range(ncol_m)], tn_m, "sigmoid",
                                 p_gq, 3 * w, p_kv, 0, 2 * kvh, p_gq, w, attn_sinks[l], seq_len, tm_m)

        y_a = _rwkv_chunk(prep, p_gq, 0, gn_w[l], gn_b[l], batch, seq_len)

        mem_n = _rmsnorm(mem2d, g_mem[l], 512)
        kv_c = _proj(mem_n, w_mem_kv, l, blocks(0, 2 * XA_WIDTH), tn, act=None, out_dtype=BF16, tm=1024,
                     name="mem_kv")
        y_c = _xattn(p_qc, 0, kv_c, p_gq, w + SWA_WIDTH, batch, seq_len, 512)

        merged = _merge(y_a, y_b, y_c, p_merge, w_up_rwkv, w_up_swa, w_up_xattn, l, 1024, 1024)
        x2d, h = _out_proj(merged, w_out, l, x2d, g_post[l], g_pre[l + 1] if l + 1 < DEPTH else None, 512)
    return x2d.reshape(batch, seq_len, d)
```

```python
import functools

import jax
import jax.numpy as jnp
import numpy as np
from jax import lax
from jax.experimental import pallas as pl
from jax.experimental.pallas import tpu as pltpu

F32 = jnp.float32
BF16 = jnp.bfloat16

D_MODEL = 2048
DEPTH = 2
MEM_LEN = 256
NORM_EPS = 1e-6
RWKV_WIDTH = 1024
RWKV_HEAD = 64
DECAY_RANK = 64
ICLR_RANK = 64
GN_EPS = 64e-5
SWA_HEAD = 64
SWA_Q_HEADS = 16
SWA_KV_HEADS = 2
SWA_WIDTH = SWA_Q_HEADS * SWA_HEAD
BLOCK = 128
XA_HEADS = 4
XA_HEAD = 256
XA_WIDTH = XA_HEADS * XA_HEAD
N_BRANCH = 3
SHIFT_COLS = 3 * RWKV_WIDTH + DECAY_RANK + ICLR_RANK

LANES = 128
CHUNK = 64
PAIR = 2 * RWKV_HEAD
VMEM_LIMIT = 56 * 1024 * 1024

_NT = (((1,), (1,)), ((), ()))
_TN = (((0,), (0,)), ((), ()))


def _params(*sem):
    return pltpu.CompilerParams(dimension_semantics=sem, vmem_limit_bytes=VMEM_LIMIT)


def _sigmoid(x):
    return 1.0 / (1.0 + jnp.exp(-x))


def _split3(x):
    hi = x.astype(BF16)
    r1 = x - hi.astype(F32)
    mid = r1.astype(BF16)
    lo = (r1 - mid.astype(F32)).astype(BF16)
    return hi, mid, lo


def _dot_exact_lhs(m, x):
    hi, mid, lo = _split3(x)
    d = lambda a: jnp.dot(m, a, preferred_element_type=F32)
    return d(hi) + d(mid) + d(lo)


def _rmsnorm_kernel(x_ref, g_ref, o_ref):
    xf = x_ref[...]
    ms = jnp.mean(xf * xf, axis=-1, keepdims=True)
    o_ref[...] = (xf * lax.rsqrt(ms + NORM_EPS) * g_ref[...]).astype(o_ref.dtype)


def _rmsnorm(x2d, g, tm):
    m, d = x2d.shape
    tm = min(tm, m)
    return pl.pallas_call(
        _rmsnorm_kernel,
        out_shape=jax.ShapeDtypeStruct((m, d), BF16),
        grid=(m // tm,),
        in_specs=[pl.BlockSpec((tm, d), lambda i: (i, 0)),
                  pl.BlockSpec((1, d), lambda i: (0, 0))],
        out_specs=pl.BlockSpec((tm, d), lambda i: (i, 0)),
        compiler_params=_params("parallel"),
        name="rmsnorm",
    )(x2d, g.reshape(1, d))


def _pick(j, values):
    out = values[0]
    for idx in range(1, len(values)):
        out = jnp.where(j == idx, values[idx], out)
    return out


def _w_spec(layer, k, tn, col_offs, single=False):
    assert all(off % LANES == 0 for off in col_offs)
    mode = dict(pipeline_mode=pl.Buffered(1)) if single else {}
    if len(col_offs) == 1:
        return pl.BlockSpec((pl.Element(1), pl.Element(k), pl.Element(tn)),
                            lambda j, i: (layer, 0, col_offs[0]), **mode)
    return pl.BlockSpec((pl.Element(1), pl.Element(k), pl.Element(tn)),
                        lambda j, i: (layer, 0, pl.multiple_of(_pick(j, col_offs), LANES)), **mode)


def _proj_kernel(h_ref, w_ref, o_ref, wb_ref):
    @pl.when(pl.program_id(1) == 0)
    def _():
        wb_ref[...] = w_ref[0].astype(BF16)

    o_ref[...] = jnp.dot(h_ref[...], wb_ref[...], preferred_element_type=F32).astype(o_ref.dtype)


def _proj(h, w, layer, col_offs, tn, *, out_dtype, tm, name):
    m, k = h.shape
    tm = min(tm, m)
    nb = len(col_offs)
    return pl.pallas_call(
        _proj_kernel,
        out_shape=jax.ShapeDtypeStruct((m, nb * tn), out_dtype),
        grid=(nb, m // tm),
        in_specs=[pl.BlockSpec((tm, k), lambda j, i: (i, 0)), _w_spec(layer, k, tn, col_offs)],
        out_specs=pl.BlockSpec((tm, tn), lambda j, i: (i, j)),
        scratch_shapes=[pltpu.VMEM((k, tn), BF16)],
        compiler_params=_params("arbitrary", "arbitrary"),
        name=name,
    )(h, w)


def _proj_small_kernel(h_ref, wq_ref, wkv_ref, wz_ref, oq_ref, okv_ref, oz_ref, wqb_ref, wkvb_ref, wzb_ref,
                       *, q_scale):
    @pl.when(pl.program_id(0) == 0)
    def _():
        wqb_ref[...] = (wq_ref[0] * q_scale).astype(BF16)
        wzb_ref[...] = wz_ref[0].astype(BF16)
        wf = wkv_ref[0]
        head0 = lax.broadcasted_iota(jnp.int32, (wf.shape[0], LANES), 1) < SWA_HEAD
        for g in range(wf.shape[1] // LANES):
            a = wf[:, g * LANES:(g + 1) * LANES]
            ra = pltpu.roll(a, SWA_HEAD, 1)
            wkvb_ref[:, 2 * g * LANES:(2 * g + 1) * LANES] = jnp.where(head0, a, ra).astype(BF16)
            wkvb_ref[:, (2 * g + 1) * LANES:(2 * g + 2) * LANES] = jnp.where(head0, ra, a).astype(BF16)

    h = h_ref[...]
    oq_ref[...] = jnp.dot(h, wqb_ref[...], preferred_element_type=F32).astype(oq_ref.dtype)
    okv_ref[...] = jnp.dot(h, wkvb_ref[...], preferred_element_type=F32).astype(okv_ref.dtype)
    oz_ref[...] = jnp.dot(h, wzb_ref[...], preferred_element_type=F32)


def _proj_small(h, w, layer, q_off, q_scale, kv_off, z_off, tm):
    m, k = h.shape
    tm = min(tm, m)
    wq, wkv, wz = XA_WIDTH, 2 * SWA_KV_HEADS * SWA_HEAD, DECAY_RANK + ICLR_RANK
    win = lambda width, off: pl.BlockSpec((pl.Element(1), pl.Element(k), pl.Element(width)),
                                          lambda i: (layer, 0, off), pipeline_mode=pl.Buffered(1))
    tok = lambda width: pl.BlockSpec((tm, width), lambda i: (i, 0))
    return pl.pallas_call(
        functools.partial(_proj_small_kernel, q_scale=q_scale),
        out_shape=[jax.ShapeDtypeStruct((m, wq), BF16), jax.ShapeDtypeStruct((m, 2 * wkv), BF16),
                   jax.ShapeDtypeStruct((m, wz), F32)],
        grid=(m // tm,),
        in_specs=[tok(k), win(wq, q_off), win(wkv, kv_off), win(wz, z_off)],
        out_specs=[tok(wq), tok(2 * wkv), tok(wz)],
        scratch_shapes=[pltpu.VMEM((k, wq), BF16), pltpu.VMEM((k, 2 * wkv), BF16), pltpu.VMEM((k, wz), BF16)],
        compiler_params=_params("arbitrary"),
        name="proj_small",
    )(h, w, w, w)


def _prep_stages(first, p_ref, prev_ref, z_ref, zprev_ref, mu_ref, wst_ref, wlo_ref, dbase_ref, ibase_ref,
                 kk_ref, ka_ref, rk_ref, tri_ref, ones_ref,
                 at_ref, rt_ref, bt_ref, kt_ref, bd_ref, kd_ref, v_ref, bonus_ref, wc_ref):
    tp = p_ref.shape[0]
    w = RWKV_WIDTH
    gw = 2 * LANES
    groups = [slice(g * gw, (g + 1) * gw) for g in range(w // gw)]

    def shifted(c0, width, cur_ref=p_ref, before_ref=prev_ref, mu0=0):
        cols = slice(c0, c0 + width)
        x = cur_ref[:, cols]
        xs = pltpu.roll(x, 1, 0)
        prev_last = jnp.where(first, 0.0, before_ref[7:8, cols])
        row = lax.broadcasted_iota(jnp.int32, (8, width), 0)
        xprev = jnp.concatenate([jnp.where(row == 0, prev_last, xs[:8]), xs[8:]], axis=0)
        return x + mu_ref[:, mu0 + c0:mu0 + c0 + width] * (xprev - x)

    def dot(a, b):
        return jnp.dot(a, b, preferred_element_type=F32)

    def split2(x):
        hi = x.astype(BF16)
        return hi, (x - hi.astype(F32)).astype(BF16)

    def head_sum(t):
        hi, lo = split2(t)
        return dot(hi, ones_ref[...]) + dot(lo, ones_ref[...])

    z = shifted(0, LANES, z_ref, zprev_ref, 3 * w)
    lane = lax.broadcasted_iota(jnp.int32, z.shape, 1)
    z = jnp.where(lane < DECAY_RANK, jnp.tanh(z), z)
    z_hi, z_lo = split2(z)
    zz = jnp.concatenate([z_hi, z_lo], axis=1)
    up = lambda c0: dot(zz, wst_ref[:, c0:c0 + gw]) + dot(z_hi, wlo_ref[:, c0:c0 + gw])
    decs = [up(cols.start) for cols in groups]
    apres = [up(w + cols.start) for cols in groups]
    yield

    a_s, logws = [], []
    for cols, dec, apre in zip(groups, decs, apres):
        nz = -(dbase_ref[:, cols] + dec)
        a_s.append(_sigmoid(ibase_ref[:, cols] + apre))
        softplus = jnp.maximum(nz, 0.0) + jnp.log(1.0 + jnp.exp(-jnp.abs(nz)))
        logws.append(-jnp.exp(-softplus - 0.5))
    css = [_dot_exact_lhs(tri_ref[...], logw) for logw in logws]
    yield

    rs = [shifted(cols.start, gw) for cols in groups]
    ks = [shifted(w + cols.start, gw) for cols in groups]
    vs = [shifted(2 * w + cols.start, gw) for cols in groups]
    kks = [k * kk_ref[:, cols] for k, cols in zip(ks, groups)]
    kps = [k * (1.0 + (a - 1.0) * ka_ref[:, cols]) for k, a, cols in zip(ks, a_s, groups)]
    norms = [head_sum(kk * kk) for kk in kks]
    coefs = [head_sum(r * kp * rk_ref[:, cols]) for r, kp, cols in zip(rs, kps, groups)]
    yield

    for g, cols in enumerate(groups):
        cs, logw, a, kp, v = css[g], logws[g], a_s[g], kps[g], vs[g]
        lasts = [cs[c * CHUNK + CHUNK - 1:c * CHUNK + CHUNK, :] for c in range(tp // CHUNK)]
        cs_last = jnp.concatenate([jnp.broadcast_to(t, (CHUNK, gw)) for t in lasts], axis=0)
        kk = kks[g] * jnp.minimum(lax.rsqrt(norms[g]), 1e12)
        b = kk * a
        e_neg = jnp.exp(-cs)
        e_d = jnp.exp(cs_last - cs)
        at_ref[:, cols] = (-kk * jnp.exp(cs - logw)).astype(BF16)
        rt_ref[:, cols] = (rs[g] * jnp.exp(cs)).astype(BF16)
        bt_ref[:, cols] = (b * e_neg).astype(BF16)
        kt_ref[:, cols] = (kp * e_neg).astype(BF16)
        bd_ref[:, cols] = (b * e_d).astype(BF16)
        kd_ref[:, cols] = (kp * e_d).astype(BF16)
        v_ref[:, cols] = v.astype(BF16)
        bonus_ref[:, cols] = coefs[g] * v
        for c, t in enumerate(lasts):
            wc_ref[0, c:c + 1, cols] = jnp.exp(t)


N_PREP_IN, N_PREP_OUT = 14, 9


def _proj_prep_kernel(h_ref, w_ref, *refs, scales, n_silu, n_split, blocks_per_seq):
    prep_in = refs[:N_PREP_IN]
    o_ref = refs[N_PREP_IN]
    prep_out = refs[N_PREP_IN + 1:N_PREP_IN + 1 + N_PREP_OUT]
    wb_ref = refs[-1]
    j = pl.program_id(0)

    @pl.when(pl.program_id(1) == 0)
    def _():
        wb_ref[...] = (w_ref[0] * _pick(j, scales)).astype(BF16)

    step = j * pl.num_programs(1) + pl.program_id(1)
    stages = _prep_stages(step % blocks_per_seq == 0, *prep_in, *prep_out)
    cw = wb_ref.shape[1] // n_split
    next(stages)
    for s in range(n_split):
        cs_ = slice(s * cw, (s + 1) * cw)
        acc = jnp.dot(h_ref[...], wb_ref[:, cs_], preferred_element_type=F32)
        o_ref[:, cs_] = jnp.where(j < n_silu, acc * _sigmoid(acc), acc).astype(o_ref.dtype)
        next(stages, None)
    for _ in stages:
        pass


def _proj_prep(h, w_in, layer, col_offs, tn, scales, n_silu, p, pz, seq_len, mu, wda, dbase, ibase,
               k_k, k_a, r_k, tm):
    n, k = h.shape
    w = RWKV_WIDTH
    tm = min(tm, n)
    ncol, nrow = len(col_offs), n // tm
    tp = n // (ncol * nrow)
    assert tp % CHUNK == 0 and seq_len % tp == 0 and tp % 8 == 0
    cpb = tp // CHUNK
    tri = np.tril(np.ones((CHUNK, CHUNK), np.float32))
    tri = jnp.asarray(np.kron(np.eye(cpb, dtype=np.float32), tri), BF16)
    ones_bd = jnp.asarray(np.kron(np.eye(2 * LANES // RWKV_HEAD, dtype=np.float32),
                                  np.ones((RWKV_HEAD, RWKV_HEAD), np.float32)), BF16)
    w_hi = wda.astype(BF16)
    w_lo = (wda - w_hi.astype(F32)).astype(BF16)
    w_st = jnp.concatenate([w_hi, w_hi], axis=0)
    row = lambda a: a.reshape(1, -1)
    full = lambda shape: pl.BlockSpec(shape, lambda j, i: (0,) * len(shape))
    step = lambda j, i: j * nrow + i
    tok = lambda width: pl.BlockSpec((tp, width), lambda j, i: (step(j, i), 0))
    before = lambda width: pl.BlockSpec(
        (8, width), lambda j, i: (jnp.maximum(step(j, i) * (tp // 8) - 1, 0), 0))
    outs = pl.pallas_call(
        functools.partial(_proj_prep_kernel, scales=scales, n_silu=n_silu, n_split=tn // (2 * LANES),
                          blocks_per_seq=seq_len // tp),
        out_shape=[jax.ShapeDtypeStruct((n, ncol * tn), BF16)]
        + [jax.ShapeDtypeStruct((n, w), BF16)] * 7
        + [jax.ShapeDtypeStruct((n, w), F32), jax.ShapeDtypeStruct((n // tp, cpb, w), F32)],
        grid=(ncol, nrow),
        in_specs=[pl.BlockSpec((tm, k), lambda j, i: (i, 0)), _w_spec(layer, k, tn, col_offs, single=True),
                  tok(3 * w), before(3 * w), tok(LANES), before(LANES),
                  full((1, SHIFT_COLS)), full((2 * LANES, 2 * w)), full((LANES, 2 * w)),
                  full((1, w)), full((1, w)), full((1, w)), full((1, w)), full((1, w)),
                  full((tp, tp)), full((2 * LANES, 2 * LANES))],
        out_specs=[pl.BlockSpec((tm, tn), lambda j, i: (i, j))] + [tok(w)] * 8
        + [pl.BlockSpec((1, cpb, w), lambda j, i: (step(j, i), 0, 0))],
        scratch_shapes=[pltpu.VMEM((k, tn), BF16)],
        compiler_params=_params("arbitrary", "arbitrary"),
        name="proj_gq_prep",
    )(h, w_in, p, p, pz, pz, row(mu), w_st, w_lo, row(dbase), row(ibase), row(k_k), row(k_a), row(r_k),
      tri, ones_bd)
    return outs[0], outs[1:]


def _rwkv_chunk_kernel(at_ref, rt_ref, bt_ref, kt_ref, bd_ref, kd_ref, v_ref, bonus_ref, wc_ref,
                       gate_ref, gnw_ref, gnb_ref, avg_ref, o_ref,
                       pt_ref, qt_ref, g_ref, y0_ref, y_ref, *, n_chunks, n_pairs):
    c2 = 2 * CHUNK
    lane = lax.broadcasted_iota(jnp.int32, (CHUNK, PAIR), 1)
    head0 = lane < RWKV_HEAD
    ri = lax.broadcasted_iota(jnp.int32, (2 * c2, 2 * c2), 0)
    ci = lax.broadcasted_iota(jnp.int32, (2 * c2, 2 * c2), 1)
    keep = (ci & (CHUNK - 1)) < (ri & (CHUNK - 1)) + jnp.where(ri < c2, 0, 1)
    eye = (lax.broadcasted_iota(jnp.int32, (c2, c2), 0)
           == lax.broadcasted_iota(jnp.int32, (c2, c2), 1)).astype(F32)
    zeros_b = jnp.zeros((c2, c2), BF16)

    def stacked(ref, sl, ls):
        x = ref[sl, ls]
        zero = jnp.zeros_like(x)
        return jnp.concatenate([jnp.where(head0, x, zero), jnp.where(head0, zero, x)], axis=0)

    def dot(a, b):
        return jnp.dot(a, b, preferred_element_type=F32)

    def phase1_stages(items):
        def ld(ref, it):
            p, c = it
            return stacked(ref, pl.ds(pl.multiple_of(c * CHUNK, CHUNK), CHUNK),
                           slice(p * PAIR, (p + 1) * PAIR))

        lms = [lax.dot_general(jnp.concatenate([ld(at_ref, it), ld(rt_ref, it)], axis=0),
                               jnp.concatenate([ld(bt_ref, it), ld(kt_ref, it)], axis=0), _NT,
                               preferred_element_type=F32) for it in items]
        lms = [jnp.where(keep, lm, 0.0) for lm in lms]
        tops = [lm[:c2, :].astype(BF16) for lm in lms]
        m_rs = [lm[c2:, :].astype(BF16) for lm in lms]
        yield
        invs = [eye + lm[:c2, :c2] for lm in lms]
        firsts = [dot(top, jnp.concatenate([jnp.concatenate([top[:, :c2], zeros_b], axis=1),
                                            jnp.concatenate([zeros_b, ld(v_ref, it)], axis=1)], axis=0))
                  for top, it in zip(tops, items)]
        lps = [f[:, :c2] for f in firsts]
        lak_vs = [f[:, c2:].astype(BF16) for f in firsts]
        yield
        n_fac = int(np.log2(CHUNK)) - 1
        for f in range(n_fac - 1):
            lp_bs = [lp.astype(BF16) for lp in lps]
            prs = [dot(lp_b, jnp.concatenate([lp_b, inv.astype(BF16)], axis=1))
                   for lp_b, inv in zip(lp_bs, invs)]
            lps = [pr[:, :c2] for pr in prs]
            invs = [inv + pr[:, c2:] for inv, pr in zip(invs, prs)]
            yield
        invs = [inv + dot(lp.astype(BF16), inv.astype(BF16)) for lp, inv in zip(lps, invs)]
        yield
        xu_bs = [dot(inv.astype(BF16), jnp.concatenate([ld(at_ref, it), lak_v], axis=1)).astype(BF16)
                 for inv, lak_v, it in zip(invs, lak_vs, items)]
        rhss = [jnp.concatenate([xu_b, jnp.concatenate([zeros_b, ld(v_ref, it)], axis=1)], axis=0)
                for xu_b, it in zip(xu_bs, items)]
        yield
        gys = [dot(m_r, rhs) for m_r, rhs in zip(m_rs, rhss)]
        yield
        pqs = [lax.dot_general(rhs, jnp.concatenate([ld(bd_ref, it), ld(kd_ref, it)], axis=0), _TN,
                               preferred_element_type=F32)
               for rhs, it in zip(rhss, items)]
        for (p, c), gy, pq in zip(items, gys, pqs):
            g_ref[p, c] = (ld(rt_ref, (p, c)).astype(F32) + gy[:, :c2]).astype(BF16)
            y0_ref[p, c] = gy[:, c2:]
            pt_ref[p, c] = pq[:c2, :].astype(BF16)
            qt_ref[p, c] = pq[c2:, :]

    def phase2_chunk(c, sts):
        sl = pl.ds(pl.multiple_of(c * CHUNK, CHUNK), CHUNK)
        new = []
        for p in range(n_pairs):
            ls = slice(p * PAIR, (p + 1) * PAIR)
            st = sts[p]
            st_b = st.astype(BF16)
            y_d = lax.dot_general(g_ref[p, c], st_b, _NT, preferred_element_type=F32) + y0_ref[p, c]
            y_ref[sl, ls] = y_d[:CHUNK, :] + y_d[CHUNK:, :]
            new.append(st * wc_ref[0, c, :, ls] + dot(st_b, pt_ref[p, c]) + qt_ref[p, c])
        return tuple(new)

    group = 4
    n_groups = n_chunks // group

    def run_group(i, sts, build, scan):
        stages = (phase1_stages([(p, i * group + u) for p in range(n_pairs) for u in range(group)])
                  if build else iter(()))
        todo = [(i - 1) * group + u for u in range(group)] if scan else []
        for s, _ in enumerate(stages):
            if todo and s % 2 == 1:
                sts = phase2_chunk(todo.pop(0), sts)
        for c in todo:
            sts = phase2_chunk(c, sts)
        return sts

    sts = tuple(jnp.zeros((c2, c2), F32) for _ in range(n_pairs))
    sts = run_group(0, sts, True, False)
    sts = lax.fori_loop(1, n_groups, lambda i, s: run_group(i, s, True, True), sts)

    rows3 = 4 * CHUNK
    head0_r = lax.broadcasted_iota(jnp.int32, (rows3, PAIR), 1) < RWKV_HEAD

    def head_mean(t):
        s0 = jnp.sum(jnp.where(head0_r, t, 0.0), axis=-1, keepdims=True)
        s1 = jnp.sum(jnp.where(head0_r, 0.0, t), axis=-1, keepdims=True)
        return jnp.where(head0_r, s0, s1) * (1.0 / RWKV_HEAD)

    def head_mean_mxu(t):
        hi = t.astype(BF16)
        lo = (t - hi.astype(F32)).astype(BF16)
        return dot(hi, avg_ref[...]) + dot(lo, avg_ref[...])

    unroll3 = 2
    trip_rows = rows3 * unroll3

    def phase3(t):
        items = [(pl.ds(t * trip_rows + u * rows3, rows3), slice(p * PAIR, (p + 1) * PAIR))
                 for u in range(unroll3) for p in range(n_pairs)]
        ys = [y_ref[sl, ls] for sl, ls in items]
        ds = [y - head_mean(y) for y in ys]
        vs = [head_mean_mxu(d * d) for d in ds]
        for (sl, ls), d, var in zip(items, ds, vs):
            yn = d * lax.rsqrt(var + GN_EPS) * gnw_ref[:, ls] + gnb_ref[:, ls]
            o_ref[sl, ls] = ((yn + bonus_ref[sl, ls]) * gate_ref[sl, ls].astype(F32)).astype(o_ref.dtype)

    tail = [(n_groups - 1) * group + u for u in range(group)]
    trips = list(range(n_chunks * CHUNK // trip_rows))
    while trips and (trips[0] + 1) * trip_rows <= tail[0] * CHUNK:
        if tail[1:]:
            sts = phase2_chunk(tail.pop(0), sts)
        phase3(trips.pop(0))
    for c in tail:
        sts = phase2_chunk(c, sts)
    for t in trips:
        phase3(t)


def _rwkv_chunk(prep, gates, gate_col, gn_w, gn_b, batch, seq_len):
    at, rt, bt, kt, bd, kd, v, bonus, wc = prep
    n, w = at.shape
    n_chunks = seq_len // CHUNK
    n_pairs = 2
    bw = n_pairs * PAIR
    wc = wc.reshape(batch, n_chunks, 1, w)
    avg = jnp.asarray(np.kron(np.eye(PAIR // RWKV_HEAD, dtype=np.float32),
                              np.full((RWKV_HEAD, RWKV_HEAD), 1.0 / RWKV_HEAD, np.float32)), BF16)
    tok = pl.BlockSpec((seq_len, bw), lambda b, h: (b, h))
    gate_spec = pl.BlockSpec((seq_len, bw), lambda b, h: (b, gate_col // bw + h))
    vec = pl.BlockSpec((1, bw), lambda b, h: (0, h))
    c2 = 2 * CHUNK
    ops = lambda dt: pltpu.VMEM((n_pairs, n_chunks, c2, c2), dt)
    return pl.pallas_call(
        functools.partial(_rwkv_chunk_kernel, n_chunks=n_chunks, n_pairs=n_pairs),
        out_shape=jax.ShapeDtypeStruct((n, w), BF16),
        grid=(batch, w // bw),
        in_specs=[tok] * 8 + [pl.BlockSpec((1, n_chunks, 1, bw), lambda b, h: (b, 0, 0, h)),
                              gate_spec, vec, vec,
                              pl.BlockSpec((PAIR, PAIR), lambda b, h: (0, 0))],
        out_specs=tok,
        scratch_shapes=[ops(BF16), ops(F32), ops(BF16), ops(F32), pltpu.VMEM((seq_len, bw), F32)],
        compiler_params=_params("parallel", "parallel"),
        name="rwkv_chunk",
    )(at, rt, bt, kt, bd, kd, v, bonus, wc, gates, gn_w.reshape(1, w), gn_b.reshape(1, w), avg)


def _swa_stages(n, sink_ref, q_ref, kp_ref, kc_ref, vp_ref, vc_ref, g_ref, o_ref):
    lane = lax.broadcasted_iota(jnp.int32, (2 * BLOCK, LANES), 1)
    head0 = lane < SWA_HEAD
    head0_q = lax.broadcasted_iota(jnp.int32, (BLOCK, LANES), 1) < SWA_HEAD
    r = lax.broadcasted_iota(jnp.int32, (BLOCK, 2 * BLOCK), 0)
    c = lax.broadcasted_iota(jnp.int32, (BLOCK, 2 * BLOCK), 1)
    valid = (c > r) & (c <= r + BLOCK) & ((c >= BLOCK) | (n > 0))
    pairs_per_kv = SWA_Q_HEADS // SWA_KV_HEADS // 2

    def block_diag(prev, cur):
        x = jnp.concatenate([prev, cur], axis=0)
        zero = jnp.zeros_like(x)
        return jnp.concatenate([jnp.where(head0, x, zero), jnp.where(head0, zero, x)], axis=0)

    kv_heads = range(SWA_KV_HEADS)
    lanes = [slice(h * LANES, (h + 1) * LANES) for h in kv_heads]
    kblks = [block_diag(kp_ref[:, hs], kc_ref[:, hs]) for hs in lanes]
    pairs = [[h * pairs_per_kv + j for j in range(pairs_per_kv)] for h in kv_heads]
    cols = [[slice(pair * LANES, (pair + 1) * LANES) for pair in ps_] for ps_ in pairs]
    ss = [[lax.dot_general(q_ref[:, cs_], kblks[h], _NT, preferred_element_type=F32) for cs_ in cols[h]]
          for h in kv_heads]
    yield

    def softmax(h):
        heads = [(i, e) for i in range(pairs_per_kv) for e in range(2)]
        tiles = [jnp.where(valid, ss[h][i][:, e * 2 * BLOCK:(e + 1) * 2 * BLOCK], -jnp.inf)
                 for i, e in heads]
        sinks = [sink_ref[2 * pairs[h][i] + e] for i, e in heads]
        ms = [jnp.maximum(jnp.max(t, axis=-1, keepdims=True), sk) for t, sk in zip(tiles, sinks)]
        exs = [jnp.exp(t - m) for t, m in zip(tiles, ms)]
        dens = [jnp.sum(ex, axis=-1, keepdims=True) + jnp.exp(sk - m)
                for ex, sk, m in zip(exs, sinks, ms)]
        return exs, dens

    def attend(h, exs, dens):
        vblk = block_diag(vp_ref[:, lanes[h]], vc_ref[:, lanes[h]])
        outs = [jnp.dot(jnp.concatenate([exs[2 * i].astype(BF16), exs[2 * i + 1].astype(BF16)], axis=1),
                        vblk, preferred_element_type=F32) for i in range(pairs_per_kv)]
        for i, cs_ in enumerate(cols[h]):
            rden = jnp.where(head0_q, 1.0 / dens[2 * i], 1.0 / dens[2 * i + 1])
            o_ref[:, cs_] = (outs[i] * rden * g_ref[:, cs_].astype(F32)).astype(o_ref.dtype)

    probs = softmax(0)
    for h in kv_heads:
        yield
        attend(h, *probs)
        if h + 1 < SWA_KV_HEADS:
            probs = softmax(h + 1)


def _proj_swa_kernel(h_ref, w_ref, sink_ref, q_ref, kp_ref, kc_ref, vp_ref, vc_ref, g_ref,
                     o_ref, oswa_ref, wb_ref, *, act, n_split, blocks_per_seq):
    @pl.when(pl.program_id(1) == 0)
    def _():
        wb_ref[...] = w_ref[0].astype(BF16)

    step = pl.program_id(0) * pl.num_programs(1) + pl.program_id(1)
    stages = _swa_stages(step % blocks_per_seq, sink_ref, q_ref, kp_ref, kc_ref, vp_ref, vc_ref,
                         g_ref, oswa_ref)
    cw = wb_ref.shape[1] // n_split
    next(stages)
    for s in range(n_split):
        cs_ = slice(s * cw, (s + 1) * cw)
        acc = jnp.dot(h_ref[...], wb_ref[:, cs_], preferred_element_type=F32)
        if act == "sigmoid":
            acc = _sigmoid(acc)
        o_ref[:, cs_] = acc.astype(o_ref.dtype)
        next(stages, None)
    for _ in stages:
        pass


def _proj_swa(h, w, layer, col_offs, tn, act, plain, q_col, kv, k_col, v_col, gates, gate_col, sinks,
              seq_len, tm):
    m, k = h.shape
    tm = min(tm, m)
    ncol, nrow = len(col_offs), m // tm
    assert ncol * nrow * BLOCK == m, "one attention block per projection grid step"
    w_swa = SWA_WIDTH
    nb = seq_len // BLOCK
    kvw = 2 * SWA_KV_HEADS * SWA_HEAD
    step = lambda j, i: j * nrow + i
    cur = lambda col, width: (lambda j, i: (step(j, i), col // width))
    prev = lambda col, width: (
        lambda j, i: (jnp.where(step(j, i) % nb == 0, step(j, i), step(j, i) - 1), col // width))
    return pl.pallas_call(
        functools.partial(_proj_swa_kernel, act=act, n_split=tn // (2 * LANES), blocks_per_seq=nb),
        out_shape=[jax.ShapeDtypeStruct((m, ncol * tn), BF16), jax.ShapeDtypeStruct((m, w_swa), BF16)],
        grid=(ncol, nrow),
        in_specs=[pl.BlockSpec((tm, k), lambda j, i: (i, 0)), _w_spec(layer, k, tn, col_offs),
                  pl.BlockSpec(memory_space=pltpu.SMEM),
                  pl.BlockSpec((BLOCK, w_swa), cur(q_col, w_swa)),
                  pl.BlockSpec((BLOCK, kvw), prev(k_col, kvw)), pl.BlockSpec((BLOCK, kvw), cur(k_col, kvw)),
                  pl.BlockSpec((BLOCK, kvw), prev(v_col, kvw)), pl.BlockSpec((BLOCK, kvw), cur(v_col, kvw)),
                  pl.BlockSpec((BLOCK, w_swa), cur(gate_col, w_swa))],
        out_specs=[pl.BlockSpec((tm, tn), lambda j, i: (i, j)),
                   pl.BlockSpec((BLOCK, w_swa), cur(0, w_swa))],
        scratch_shapes=[pltpu.VMEM((k, tn), BF16)],
        compiler_params=_params("arbitrary", "arbitrary"),
        name="proj_merge_swa",
    )(h, w, sinks, plain, kv, kv, kv, kv, gates)


def _xattn_kernel(q_ref, kv_ref, g_ref, o_ref):
    for h in range(XA_HEADS):
        hs = slice(h * XA_HEAD, (h + 1) * XA_HEAD)
        vs = slice(XA_WIDTH + h * XA_HEAD, XA_WIDTH + (h + 1) * XA_HEAD)
        s = lax.dot_general(q_ref[:, hs], kv_ref[:, hs], _NT, preferred_element_type=F32)
        m = jnp.max(s, axis=-1, keepdims=True)
        ex = jnp.exp(s - m)
        p = (ex / jnp.sum(ex, axis=-1, keepdims=True)).astype(BF16)
        o = jnp.dot(p, kv_ref[:, vs], preferred_element_type=F32)
        o_ref[:, hs] = (o * g_ref[:, hs].astype(F32)).astype(o_ref.dtype)


def _xattn(plain, q_col, kv, gates, gate_col, batch, seq_len, tq):
    n = plain.shape[0]
    w = XA_WIDTH
    tq = min(tq, seq_len)
    nq = seq_len // tq
    m = kv.shape[0] // batch
    tok = lambda col: pl.BlockSpec((tq, w), lambda b, i: (b * nq + i, col // w))
    return pl.pallas_call(
        _xattn_kernel,
        out_shape=jax.ShapeDtypeStruct((n, w), BF16),
        grid=(batch, nq),
        in_specs=[tok(q_col), pl.BlockSpec((m, kv.shape[1]), lambda b, i: (b, 0)), tok(gate_col)],
        out_specs=tok(0),
        compiler_params=_params("parallel", "parallel"),
        name="xattn",
    )(plain, kv, gates)


def _merge_kernel(ya_ref, yb_ref, yc_ref, ga_ref, gb_ref, gc_ref, wa_ref, wb_ref, wc_ref, o_ref,
                  wab_ref, wbb_ref, wcb_ref):
    @pl.when(pl.program_id(1) == 0)
    def _():
        wab_ref[...] = wa_ref[0].astype(BF16)
        wbb_ref[...] = wb_ref[0].astype(BF16)
        wcb_ref[...] = wc_ref[0].astype(BF16)

    d = lambda y, w: jnp.dot(y[...], w[...], preferred_element_type=F32)
    acc = ga_ref[...].astype(F32) * d(ya_ref, wab_ref)
    acc += gb_ref[...].astype(F32) * d(yb_ref, wbb_ref)
    acc += gc_ref[...].astype(F32) * d(yc_ref, wcb_ref)
    o_ref[...] = acc.astype(o_ref.dtype)


def _merge(ya, yb, yc, gates, wa, wb, wc, layer, tm, tn):
    n = ya.shape[0]
    d = wa.shape[2]
    tm = min(tm, n)
    nj = d // tn
    y_spec = lambda a: pl.BlockSpec((tm, a.shape[1]), lambda j, i: (i, 0))
    w_spec = lambda a: pl.BlockSpec((1, a.shape[1], tn), lambda j, i: (layer, 0, j),
                                    pipeline_mode=pl.Buffered(1))
    g_spec = lambda br: pl.BlockSpec((tm, tn), lambda j, i: (i, br * nj + j))
    return pl.pallas_call(
        _merge_kernel,
        out_shape=jax.ShapeDtypeStruct((n, d), BF16),
        grid=(nj, n // tm),
        in_specs=[y_spec(ya), y_spec(yb), y_spec(yc), g_spec(0), g_spec(1), g_spec(2),
                  w_spec(wa), w_spec(wb), w_spec(wc)],
        out_specs=pl.BlockSpec((tm, tn), lambda j, i: (i, j)),
        scratch_shapes=[pltpu.VMEM((a.shape[1], tn), BF16) for a in (wa, wb, wc)],
        compiler_params=_params("arbitrary", "arbitrary"),
        name="merge",
    )(ya, yb, yc, gates, gates, gates, wa, wb, wc)


def _out_kernel(m_ref, w_ref, x_ref, g_ref, *rest):
    gn_ref, o_ref, hn_ref, wb_ref = rest if len(rest) == 4 else (None, rest[0], None, rest[1])

    @pl.when(pl.program_id(0) == 0)
    def _():
        wb_ref[...] = w_ref[0].astype(BF16)

    o = jnp.dot(m_ref[...], wb_ref[...], preferred_element_type=F32)
    ms = jnp.mean(o * o, axis=-1, keepdims=True)
    xn = x_ref[...] + o * lax.rsqrt(ms + NORM_EPS) * g_ref[...]
    o_ref[...] = xn
    if hn_ref is not None:
        ms_n = jnp.mean(xn * xn, axis=-1, keepdims=True)
        hn_ref[...] = (xn * lax.rsqrt(ms_n + NORM_EPS) * gn_ref[...]).astype(hn_ref.dtype)


def _out_proj(merged, w_out, layer, x2d, g_post, g_next, tm):
    n, d = x2d.shape
    tm = min(tm, n)
    tok = pl.BlockSpec((tm, d), lambda i: (i, 0))
    vec = pl.BlockSpec((1, d), lambda i: (0, 0))
    nxt = g_next is not None
    out = pl.pallas_call(
        _out_kernel,
        out_shape=[jax.ShapeDtypeStruct((n, d), F32)] + [jax.ShapeDtypeStruct((n, d), BF16)] * nxt,
        grid=(n // tm,),
        in_specs=[tok, pl.BlockSpec((1, d, d), lambda i: (layer, 0, 0), pipeline_mode=pl.Buffered(1)),
                  tok, vec] + [vec] * nxt,
        out_specs=[tok] + [tok] * nxt,
        scratch_shapes=[pltpu.VMEM((d, d), BF16)],
        compiler_params=_params("arbitrary"),
        name="out_proj",
    )(merged, w_out, x2d, g_post.reshape(1, d), *([g_next.reshape(1, d)] if nxt else []))
    return (out[0], out[1]) if nxt else (out[0], None)


def kernel(x, mem, g_pre, w_in, mu_shift, decay_base, decay_up, iclr_base, iclr_up, k_k, k_a, r_k,
           gn_w, gn_b, attn_sinks, g_mem, w_mem_kv, w_up_rwkv, w_up_swa, w_up_xattn, w_out, g_post):
    batch, seq_len, d = x.shape
    n = batch * seq_len
    x2d = x.reshape(n, d)
    mem2d = mem.reshape(batch * mem.shape[1], d)
    w = RWKV_WIDTH
    kvh = SWA_KV_HEADS * SWA_HEAD
    o_gate_a = SHIFT_COLS
    o_q_b = o_gate_a + w
    o_kv_b = o_q_b + SWA_WIDTH
    o_gate_b = o_kv_b + 2 * kvh
    o_q_c = o_gate_b + SWA_WIDTH
    o_gate_c = o_q_c + XA_WIDTH
    o_merge = o_gate_c + XA_WIDTH

    tn = 1024
    blocks = lambda off, width: [off + t * tn for t in range(width // tn)]
    for l in range(DEPTH):
        wda = jnp.zeros((LANES, 2 * w), F32)
        wda = wda.at[:DECAY_RANK, :w].set(decay_up[l]).at[DECAY_RANK:, w:].set(iclr_up[l])

        if l == 0:
            h = _rmsnorm(x2d, g_pre[l], 512)
        p_rkv = _proj(h, w_in, l, blocks(0, 3 * w), tn, out_dtype=F32, tm=1024, name="proj_rkv")
        p_qc, p_kv, p_da = _proj_small(h, w_in, l, o_q_c, XA_HEAD ** -0.5, o_kv_b, 3 * w, 1024)
        p_gq, prep = _proj_prep(h, w_in, l, [o_gate_a, o_gate_b, o_gate_c, o_q_b], tn,
                                [1.0, 1.0, 1.0, SWA_HEAD ** -0.5], 3, p_rkv, p_da, seq_len, mu_shift[l], wda,
                                decay_base[l], iclr_base[l], k_k[l], k_a[l], r_k[l].reshape(-1), 1024)
        tm_m = min(1024, n)
        ncol_m = (n // BLOCK) // (n // tm_m)
        tn_m = N_BRANCH * d // ncol_m
        p_merge, y_b = _proj_swa(h, w_in, l, [o_merge + t * tn_m for t in range(ncol_m)], tn_m, "sigmoid",
                                 p_gq, 3 * w, p_kv, 0, 2 * kvh, p_gq, w, attn_sinks[l], seq_len, tm_m)

        y_a = _rwkv_chunk(prep, p_gq, 0, gn_w[l], gn_b[l], batch, seq_len)

        mem_n = _rmsnorm(mem2d, g_mem[l], 512)
        kv_c = _proj(mem_n, w_mem_kv, l, blocks(0, 2 * XA_WIDTH), tn, out_dtype=BF16, tm=1024,
                     name="mem_kv")
        y_c = _xattn(p_qc, 0, kv_c, p_gq, w + SWA_WIDTH, batch, seq_len, 512)

        merged = _merge(y_a, y_b, y_c, p_merge, w_up_rwkv, w_up_swa, w_up_xattn, l, 1024, 1024)
        x2d, h = _out_proj(merged, w_out, l, x2d, g_post[l], g_pre[l + 1] if l + 1 < DEPTH else None, 512)
    return x2d.reshape(batch, seq_len, d)
```

```python
import functools

import jax
import jax.numpy as jnp
import numpy as np
from jax import lax
from jax.experimental import pallas as pl
from jax.experimental.pallas import tpu as pltpu

F32 = jnp.float32
BF16 = jnp.bfloat16

D_MODEL = 2048
DEPTH = 2
MEM_LEN = 256
NORM_EPS = 1e-6
RWKV_WIDTH = 1024
RWKV_HEAD = 64
DECAY_RANK = 64
ICLR_RANK = 64
GN_EPS = 64e-5
SWA_HEAD = 64
SWA_Q_HEADS = 16
SWA_KV_HEADS = 2
SWA_WIDTH = SWA_Q_HEADS * SWA_HEAD
BLOCK = 128
XA_HEADS = 4
XA_HEAD = 256
XA_WIDTH = XA_HEADS * XA_HEAD
N_BRANCH = 3
SHIFT_COLS = 3 * RWKV_WIDTH + DECAY_RANK + ICLR_RANK

LANES = 128
CHUNK = 64
PAIR = 2 * RWKV_HEAD
VMEM_LIMIT = 56 * 1024 * 1024

_NT = (((1,), (1,)), ((), ()))
_TN = (((0,), (0,)), ((), ()))


def _params(*sem):
    return pltpu.CompilerParams(dimension_semantics=sem, vmem_limit_bytes=VMEM_LIMIT)


def _sigmoid(x):
    return 1.0 / (1.0 + jnp.exp(-x))


def _split3(x):
    hi = x.astype(BF16)
    r1 = x - hi.astype(F32)
    mid = r1.astype(BF16)
    lo = (r1 - mid.astype(F32)).astype(BF16)
    return hi, mid, lo


def _dot_exact_lhs(m, x):
    hi, mid, lo = _split3(x)
    d = lambda a: jnp.dot(m, a, preferred_element_type=F32)
    return d(hi) + d(mid) + d(lo)


def _rmsnorm_kernel(x_ref, g_ref, o_ref):
    xf = x_ref[...]
    ms = jnp.mean(xf * xf, axis=-1, keepdims=True)
    o_ref[...] = (xf * lax.rsqrt(ms + NORM_EPS) * g_ref[...]).astype(o_ref.dtype)


def _rmsnorm(x2d, g, tm):
    m, d = x2d.shape
    tm = min(tm, m)
    return pl.pallas_call(
        _rmsnorm_kernel,
        out_shape=jax.ShapeDtypeStruct((m, d), BF16),
        grid=(m // tm,),
        in_specs=[pl.BlockSpec((tm, d), lambda i: (i, 0)),
                  pl.BlockSpec((1, d), lambda i: (0, 0))],
        out_specs=pl.BlockSpec((tm, d), lambda i: (i, 0)),
        compiler_params=_params("parallel"),
        name="rmsnorm",
    )(x2d, g.reshape(1, d))


def _pick(j, values):
    out = values[0]
    for idx in range(1, len(values)):
        out = jnp.where(j == idx, values[idx], out)
    return out


def _w_spec(layer, k, tn, col_offs, single=False):
    assert all(off % LANES == 0 for off in col_offs)
    mode = dict(pipeline_mode=pl.Buffered(1)) if single else {}
    if len(col_offs) == 1:
        return pl.BlockSpec((pl.Element(1), pl.Element(k), pl.Element(tn)),
                            lambda j, i: (layer, 0, col_offs[0]), **mode)
    return pl.BlockSpec((pl.Element(1), pl.Element(k), pl.Element(tn)),
                        lambda j, i: (layer, 0, pl.multiple_of(_pick(j, col_offs), LANES)), **mode)


def _proj_kernel(h_ref, w_ref, o_ref, wb_ref):
    @pl.when(pl.program_id(1) == 0)
    def _():
        wb_ref[...] = w_ref[0].astype(BF16)

    o_ref[...] = jnp.dot(h_ref[...], wb_ref[...], preferred_element_type=F32).astype(o_ref.dtype)


def _proj(h, w, layer, col_offs, tn, *, out_dtype, tm, name):
    m, k = h.shape
    tm = min(tm, m)
    nb = len(col_offs)
    return pl.pallas_call(
        _proj_kernel,
        out_shape=jax.ShapeDtypeStruct((m, nb * tn), out_dtype),
        grid=(nb, m // tm),
        in_specs=[pl.BlockSpec((tm, k), lambda j, i: (i, 0)),
                  _w_spec(layer, k, tn, col_offs, single=True)],
        out_specs=pl.BlockSpec((tm, tn), lambda j, i: (i, j)),
        scratch_shapes=[pltpu.VMEM((k, tn), BF16)],
        compiler_params=_params("arbitrary", "arbitrary"),
        name=name,
    )(h, w)


def _proj_small_kernel(h_ref, wq_ref, wkv_ref, wz_ref, oq_ref, okv_ref, oz_ref, wqb_ref, wkvb_ref, wzb_ref,
                       *, q_scale):
    @pl.when(pl.program_id(0) == 0)
    def _():
        wqb_ref[...] = (wq_ref[0] * q_scale).astype(BF16)
        wzb_ref[...] = wz_ref[0].astype(BF16)
        wf = wkv_ref[0]
        head0 = lax.broadcasted_iota(jnp.int32, (wf.shape[0], LANES), 1) < SWA_HEAD
        for g in range(wf.shape[1] // LANES):
            a = wf[:, g * LANES:(g + 1) * LANES]
            ra = pltpu.roll(a, SWA_HEAD, 1)
            wkvb_ref[:, 2 * g * LANES:(2 * g + 1) * LANES] = jnp.where(head0, a, ra).astype(BF16)
            wkvb_ref[:, (2 * g + 1) * LANES:(2 * g + 2) * LANES] = jnp.where(head0, ra, a).astype(BF16)

    h = h_ref[...]
    oq_ref[...] = jnp.dot(h, wqb_ref[...], preferred_element_type=F32).astype(oq_ref.dtype)
    okv_ref[...] = jnp.dot(h, wkvb_ref[...], preferred_element_type=F32).astype(okv_ref.dtype)
    oz_ref[...] = jnp.dot(h, wzb_ref[...], preferred_element_type=F32)


def _proj_small(h, w, layer, q_off, q_scale, kv_off, z_off, tm):
    m, k = h.shape
    tm = min(tm, m)
    wq, wkv, wz = XA_WIDTH, 2 * SWA_KV_HEADS * SWA_HEAD, DECAY_RANK + ICLR_RANK
    win = lambda width, off: pl.BlockSpec((pl.Element(1), pl.Element(k), pl.Element(width)),
                                          lambda i: (layer, 0, off), pipeline_mode=pl.Buffered(1))
    tok = lambda width: pl.BlockSpec((tm, width), lambda i: (i, 0))
    return pl.pallas_call(
        functools.partial(_proj_small_kernel, q_scale=q_scale),
        out_shape=[jax.ShapeDtypeStruct((m, wq), BF16), jax.ShapeDtypeStruct((m, 2 * wkv), BF16),
                   jax.ShapeDtypeStruct((m, wz), F32)],
        grid=(m // tm,),
        in_specs=[tok(k), win(wq, q_off), win(wkv, kv_off), win(wz, z_off)],
        out_specs=[tok(wq), tok(2 * wkv), tok(wz)],
        scratch_shapes=[pltpu.VMEM((k, wq), BF16), pltpu.VMEM((k, 2 * wkv), BF16), pltpu.VMEM((k, wz), BF16)],
        compiler_params=_params("arbitrary"),
        name="proj_small",
    )(h, w, w, w)


def _prep_stages(first, p_ref, prev_ref, z_ref, zprev_ref, mu_ref, wst_ref, wlo_ref, dbase_ref, ibase_ref,
                 kk_ref, ka_ref, rk_ref, tri_ref, ones_ref,
                 at_ref, rt_ref, bt_ref, kt_ref, bd_ref, kd_ref, v_ref, bonus_ref, wc_ref):
    tp = p_ref.shape[0]
    w = RWKV_WIDTH
    gw = 2 * LANES
    groups = [slice(g * gw, (g + 1) * gw) for g in range(w // gw)]

    def shifted(c0, width, cur_ref=p_ref, before_ref=prev_ref, mu0=0):
        cols = slice(c0, c0 + width)
        x = cur_ref[:, cols]
        xs = pltpu.roll(x, 1, 0)
        prev_last = jnp.where(first, 0.0, before_ref[7:8, cols])
        row = lax.broadcasted_iota(jnp.int32, (8, width), 0)
        xprev = jnp.concatenate([jnp.where(row == 0, prev_last, xs[:8]), xs[8:]], axis=0)
        return x + mu_ref[:, mu0 + c0:mu0 + c0 + width] * (xprev - x)

    def dot(a, b):
        return jnp.dot(a, b, preferred_element_type=F32)

    def split2(x):
        hi = x.astype(BF16)
        return hi, (x - hi.astype(F32)).astype(BF16)

    def head_sum(t):
        hi, lo = split2(t)
        return dot(hi, ones_ref[...]) + dot(lo, ones_ref[...])

    z = shifted(0, LANES, z_ref, zprev_ref, 3 * w)
    lane = lax.broadcasted_iota(jnp.int32, z.shape, 1)
    z = jnp.where(lane < DECAY_RANK, jnp.tanh(z), z)
    z_hi, z_lo = split2(z)
    zz = jnp.concatenate([z_hi, z_lo], axis=1)
    up = lambda c0: dot(zz, wst_ref[:, c0:c0 + gw]) + dot(z_hi, wlo_ref[:, c0:c0 + gw])
    decs = [up(cols.start) for cols in groups]
    apres = [up(w + cols.start) for cols in groups]
    yield

    a_s, logws = [], []
    for cols, dec, apre in zip(groups, decs, apres):
        nz = -(dbase_ref[:, cols] + dec)
        a_s.append(_sigmoid(ibase_ref[:, cols] + apre))
        softplus = jnp.maximum(nz, 0.0) + jnp.log(1.0 + jnp.exp(-jnp.abs(nz)))
        logws.append(-jnp.exp(-softplus - 0.5))
    css = [_dot_exact_lhs(tri_ref[...], logw) for logw in logws]
    yield

    rs = [shifted(cols.start, gw) for cols in groups]
    ks = [shifted(w + cols.start, gw) for cols in groups]
    vs = [shifted(2 * w + cols.start, gw) for cols in groups]
    kks = [k * kk_ref[:, cols] for k, cols in zip(ks, groups)]
    kps = [k * (1.0 + (a - 1.0) * ka_ref[:, cols]) for k, a, cols in zip(ks, a_s, groups)]
    norms = [head_sum(kk * kk) for kk in kks]
    coefs = [head_sum(r * kp * rk_ref[:, cols]) for r, kp, cols in zip(rs, kps, groups)]
    yield

    for g, cols in enumerate(groups):
        cs, logw, a, kp, v = css[g], logws[g], a_s[g], kps[g], vs[g]
        lasts = [cs[c * CHUNK + CHUNK - 1:c * CHUNK + CHUNK, :] for c in range(tp // CHUNK)]
        cs_last = jnp.concatenate([jnp.broadcast_to(t, (CHUNK, gw)) for t in lasts], axis=0)
        kk = kks[g] * jnp.minimum(lax.rsqrt(norms[g]), 1e12)
        b = kk * a
        e_neg = jnp.exp(-cs)
        e_d = jnp.exp(cs_last - cs)
        at_ref[:, cols] = (-kk * jnp.exp(cs - logw)).astype(BF16)
        rt_ref[:, cols] = (rs[g] * jnp.exp(cs)).astype(BF16)
        bt_ref[:, cols] = (b * e_neg).astype(BF16)
        kt_ref[:, cols] = (kp * e_neg).astype(BF16)
        bd_ref[:, cols] = (b * e_d).astype(BF16)
        kd_ref[:, cols] = (kp * e_d).astype(BF16)
        v_ref[:, cols] = v.astype(BF16)
        bonus_ref[:, cols] = coefs[g] * v
        for c, t in enumerate(lasts):
            wc_ref[0, c:c + 1, cols] = jnp.exp(t)


N_PREP_IN, N_PREP_OUT = 14, 9


def _proj_prep_kernel(h_ref, w_ref, *refs, scales, n_silu, n_split, blocks_per_seq):
    prep_in = refs[:N_PREP_IN]
    o_ref = refs[N_PREP_IN]
    prep_out = refs[N_PREP_IN + 1:N_PREP_IN + 1 + N_PREP_OUT]
    wb_ref = refs[-1]
    j = pl.program_id(0)

    @pl.when(pl.program_id(1) == 0)
    def _():
        wb_ref[...] = (w_ref[0] * _pick(j, scales)).astype(BF16)

    step = j * pl.num_programs(1) + pl.program_id(1)
    stages = _prep_stages(step % blocks_per_seq == 0, *prep_in, *prep_out)
    cw = wb_ref.shape[1] // n_split
    next(stages)
    for s in range(n_split):
        cs_ = slice(s * cw, (s + 1) * cw)
        acc = jnp.dot(h_ref[...], wb_ref[:, cs_], preferred_element_type=F32)
        o_ref[:, cs_] = jnp.where(j < n_silu, acc * _sigmoid(acc), acc).astype(o_ref.dtype)
        next(stages, None)
    for _ in stages:
        pass


def _proj_prep(h, w_in, layer, col_offs, tn, scales, n_silu, p, pz, seq_len, mu, wda, dbase, ibase,
               k_k, k_a, r_k, tm):
    n, k = h.shape
    w = RWKV_WIDTH
    tm = min(tm, n)
    ncol, nrow = len(col_offs), n // tm
    tp = n // (ncol * nrow)
    assert tp % CHUNK == 0 and seq_len % tp == 0 and tp % 8 == 0
    cpb = tp // CHUNK
    tri = np.tril(np.ones((CHUNK, CHUNK), np.float32))
    tri = jnp.asarray(np.kron(np.eye(cpb, dtype=np.float32), tri), BF16)
    ones_bd = jnp.asarray(np.kron(np.eye(2 * LANES // RWKV_HEAD, dtype=np.float32),
                                  np.ones((RWKV_HEAD, RWKV_HEAD), np.float32)), BF16)
    w_hi = wda.astype(BF16)
    w_lo = (wda - w_hi.astype(F32)).astype(BF16)
    w_st = jnp.concatenate([w_hi, w_hi], axis=0)
    row = lambda a: a.reshape(1, -1)
    full = lambda shape: pl.BlockSpec(shape, lambda j, i: (0,) * len(shape))
    step = lambda j, i: j * nrow + i
    tok = lambda width: pl.BlockSpec((tp, width), lambda j, i: (step(j, i), 0))
    before = lambda width: pl.BlockSpec(
        (8, width), lambda j, i: (jnp.maximum(step(j, i) * (tp // 8) - 1, 0), 0))
    outs = pl.pallas_call(
        functools.partial(_proj_prep_kernel, scales=scales, n_silu=n_silu, n_split=tn // (2 * LANES),
                          blocks_per_seq=seq_len // tp),
        out_shape=[jax.ShapeDtypeStruct((n, ncol * tn), BF16)]
        + [jax.ShapeDtypeStruct((n, w), BF16)] * 7
        + [jax.ShapeDtypeStruct((n, w), F32), jax.ShapeDtypeStruct((n // tp, cpb, w), F32)],
        grid=(ncol, nrow),
        in_specs=[pl.BlockSpec((tm, k), lambda j, i: (i, 0)), _w_spec(layer, k, tn, col_offs, single=True),
                  tok(3 * w), before(3 * w), tok(LANES), before(LANES),
                  full((1, SHIFT_COLS)), full((2 * LANES, 2 * w)), full((LANES, 2 * w)),
                  full((1, w)), full((1, w)), full((1, w)), full((1, w)), full((1, w)),
                  full((tp, tp)), full((2 * LANES, 2 * LANES))],
        out_specs=[pl.BlockSpec((tm, tn), lambda j, i: (i, j))] + [tok(w)] * 8
        + [pl.BlockSpec((1, cpb, w), lambda j, i: (step(j, i), 0, 0))],
        scratch_shapes=[pltpu.VMEM((k, tn), BF16)],
        compiler_params=_params("arbitrary", "arbitrary"),
        name="proj_gq_prep",
    )(h, w_in, p, p, pz, pz, row(mu), w_st, w_lo, row(dbase), row(ibase), row(k_k), row(k_a), row(r_k),
      tri, ones_bd)
    return outs[0], outs[1:]


def _rwkv_chunk_kernel(at_ref, rt_ref, bt_ref, kt_ref, bd_ref, kd_ref, v_ref, bonus_ref, wc_ref,
                       gate_ref, gnw_ref, gnb_ref, avg_ref, o_ref,
                       pt_ref, qt_ref, g_ref, y0_ref, y_ref, *, n_chunks, n_pairs):
    c2 = 2 * CHUNK
    lane = lax.broadcasted_iota(jnp.int32, (CHUNK, PAIR), 1)
    head0 = lane < RWKV_HEAD
    ri = lax.broadcasted_iota(jnp.int32, (2 * c2, 2 * c2), 0)
    ci = lax.broadcasted_iota(jnp.int32, (2 * c2, 2 * c2), 1)
    keep = (ci & (CHUNK - 1)) < (ri & (CHUNK - 1)) + jnp.where(ri < c2, 0, 1)
    eye = (lax.broadcasted_iota(jnp.int32, (c2, c2), 0)
           == lax.broadcasted_iota(jnp.int32, (c2, c2), 1)).astype(F32)
    zeros_b = jnp.zeros((c2, c2), BF16)

    def stacked(ref, sl, ls):
        x = ref[sl, ls]
        zero = jnp.zeros_like(x)
        return jnp.concatenate([jnp.where(head0, x, zero), jnp.where(head0, zero, x)], axis=0)

    def dot(a, b):
        return jnp.dot(a, b, preferred_element_type=F32)

    def phase1_stages(items):
        def ld(ref, it):
            p, c = it
            return stacked(ref, pl.ds(pl.multiple_of(c * CHUNK, CHUNK), CHUNK),
                           slice(p * PAIR, (p + 1) * PAIR))

        lms = [lax.dot_general(jnp.concatenate([ld(at_ref, it), ld(rt_ref, it)], axis=0),
                               jnp.concatenate([ld(bt_ref, it), ld(kt_ref, it)], axis=0), _NT,
                               preferred_element_type=F32) for it in items]
        lms = [jnp.where(keep, lm, 0.0) for lm in lms]
        tops = [lm[:c2, :].astype(BF16) for lm in lms]
        m_rs = [lm[c2:, :].astype(BF16) for lm in lms]
        yield
        invs = [eye + lm[:c2, :c2] for lm in lms]
        firsts = [dot(top, jnp.concatenate([jnp.concatenate([top[:, :c2], zeros_b], axis=1),
                                            jnp.concatenate([zeros_b, ld(v_ref, it)], axis=1)], axis=0))
                  for top, it in zip(tops, items)]
        lps = [f[:, :c2] for f in firsts]
        lak_vs = [f[:, c2:].astype(BF16) for f in firsts]
        yield
        n_fac = int(np.log2(CHUNK)) - 1
        for f in range(n_fac - 1):
            lp_bs = [lp.astype(BF16) for lp in lps]
            prs = [dot(lp_b, jnp.concatenate([lp_b, inv.astype(BF16)], axis=1))
                   for lp_b, inv in zip(lp_bs, invs)]
            lps = [pr[:, :c2] for pr in prs]
            invs = [inv + pr[:, c2:] for inv, pr in zip(invs, prs)]
            yield
        invs = [inv + dot(lp.astype(BF16), inv.astype(BF16)) for lp, inv in zip(lps, invs)]
        yield
        xu_bs = [dot(inv.astype(BF16), jnp.concatenate([ld(at_ref, it), lak_v], axis=1)).astype(BF16)
                 for inv, lak_v, it in zip(invs, lak_vs, items)]
        rhss = [jnp.concatenate([xu_b, jnp.concatenate([zeros_b, ld(v_ref, it)], axis=1)], axis=0)
                for xu_b, it in zip(xu_bs, items)]
        yield
        gys = [dot(m_r, rhs) for m_r, rhs in zip(m_rs, rhss)]
        yield
        pqs = [lax.dot_general(rhs, jnp.concatenate([ld(bd_ref, it), ld(kd_ref, it)], axis=0), _TN,
                               preferred_element_type=F32)
               for rhs, it in zip(rhss, items)]
        for (p, c), gy, pq in zip(items, gys, pqs):
            g_ref[p, c] = (ld(rt_ref, (p, c)).astype(F32) + gy[:, :c2]).astype(BF16)
            y0_ref[p, c] = gy[:, c2:]
            pt_ref[p, c] = pq[:c2, :].astype(BF16)
            qt_ref[p, c] = pq[c2:, :]

    def phase2_chunk(c, sts):
        sl = pl.ds(pl.multiple_of(c * CHUNK, CHUNK), CHUNK)
        new = []
        for p in range(n_pairs):
            ls = slice(p * PAIR, (p + 1) * PAIR)
            st = sts[p]
            st_b = st.astype(BF16)
            y_d = lax.dot_general(g_ref[p, c], st_b, _NT, preferred_element_type=F32) + y0_ref[p, c]
            y_ref[sl, ls] = y_d[:CHUNK, :] + y_d[CHUNK:, :]
            new.append(st * wc_ref[0, c, :, ls] + dot(st_b, pt_ref[p, c]) + qt_ref[p, c])
        return tuple(new)

    group = 4
    n_groups = n_chunks // group

    def run_group(i, sts, scan):
        stages = phase1_stages([(p, i * group + u) for p in range(n_pairs) for u in range(group)])
        todo = [(i - 1) * group + u for u in range(group)] if scan else []
        for s, _ in enumerate(stages):
            if todo and s % 2 == 1:
                sts = phase2_chunk(todo.pop(0), sts)
        for c in todo:
            sts = phase2_chunk(c, sts)
        return sts

    sts = tuple(jnp.zeros((c2, c2), F32) for _ in range(n_pairs))
    sts = run_group(0, sts, False)
    sts = lax.fori_loop(1, n_groups, lambda i, s: run_group(i, s, True), sts)

    rows3 = 4 * CHUNK
    head0_r = lax.broadcasted_iota(jnp.int32, (rows3, PAIR), 1) < RWKV_HEAD

    def head_mean(t):
        s0 = jnp.sum(jnp.where(head0_r, t, 0.0), axis=-1, keepdims=True)
        s1 = jnp.sum(jnp.where(head0_r, 0.0, t), axis=-1, keepdims=True)
        return jnp.where(head0_r, s0, s1) * (1.0 / RWKV_HEAD)

    def head_mean_mxu(t):
        hi = t.astype(BF16)
        lo = (t - hi.astype(F32)).astype(BF16)
        return dot(hi, avg_ref[...]) + dot(lo, avg_ref[...])

    unroll3 = 2
    trip_rows = rows3 * unroll3

    def phase3(t):
        items = [(pl.ds(t * trip_rows + u * rows3, rows3), slice(p * PAIR, (p + 1) * PAIR))
                 for u in range(unroll3) for p in range(n_pairs)]
        ys = [y_ref[sl, ls] for sl, ls in items]
        ds = [y - head_mean(y) for y in ys]
        vs = [head_mean_mxu(d * d) for d in ds]
        for (sl, ls), d, var in zip(items, ds, vs):
            yn = d * lax.rsqrt(var + GN_EPS) * gnw_ref[:, ls] + gnb_ref[:, ls]
            o_ref[sl, ls] = ((yn + bonus_ref[sl, ls]) * gate_ref[sl, ls].astype(F32)).astype(o_ref.dtype)

    tail = [(n_groups - 1) * group + u for u in range(group)]
    trips = list(range(n_chunks * CHUNK // trip_rows))
    while trips and (trips[0] + 1) * trip_rows <= tail[0] * CHUNK:
        if tail[1:]:
            sts = phase2_chunk(tail.pop(0), sts)
        phase3(trips.pop(0))
    for c in tail:
        sts = phase2_chunk(c, sts)
    for t in trips:
        phase3(t)


def _rwkv_chunk(prep, gates, gate_col, gn_w, gn_b, batch, seq_len):
    at, rt, bt, kt, bd, kd, v, bonus, wc = prep
    n, w = at.shape
    n_chunks = seq_len // CHUNK
    n_pairs = 2
    bw = n_pairs * PAIR
    wc = wc.reshape(batch, n_chunks, 1, w)
    avg = jnp.asarray(np.kron(np.eye(PAIR // RWKV_HEAD, dtype=np.float32),
                              np.full((RWKV_HEAD, RWKV_HEAD), 1.0 / RWKV_HEAD, np.float32)), BF16)
    tok = pl.BlockSpec((seq_len, bw), lambda b, h: (b, h))
    gate_spec = pl.BlockSpec((seq_len, bw), lambda b, h: (b, gate_col // bw + h))
    vec = pl.BlockSpec((1, bw), lambda b, h: (0, h))
    c2 = 2 * CHUNK
    ops = lambda dt: pltpu.VMEM((n_pairs, n_chunks, c2, c2), dt)
    return pl.pallas_call(
        functools.partial(_rwkv_chunk_kernel, n_chunks=n_chunks, n_pairs=n_pairs),
        out_shape=jax.ShapeDtypeStruct((n, w), BF16),
        grid=(batch, w // bw),
        in_specs=[tok] * 8 + [pl.BlockSpec((1, n_chunks, 1, bw), lambda b, h: (b, 0, 0, h)),
                              gate_spec, vec, vec,
                              pl.BlockSpec((PAIR, PAIR), lambda b, h: (0, 0))],
        out_specs=tok,
        scratch_shapes=[ops(BF16), ops(F32), ops(BF16), ops(F32), pltpu.VMEM((seq_len, bw), F32)],
        compiler_params=_params("parallel", "parallel"),
        name="rwkv_chunk",
    )(at, rt, bt, kt, bd, kd, v, bonus, wc, gates, gn_w.reshape(1, w), gn_b.reshape(1, w), avg)


def _swa_stages(n, sink_ref, q_ref, kp_ref, kc_ref, vp_ref, vc_ref, g_ref, o_ref):
    lane = lax.broadcasted_iota(jnp.int32, (2 * BLOCK, LANES), 1)
    head0 = lane < SWA_HEAD
    head0_q = lax.broadcasted_iota(jnp.int32, (BLOCK, LANES), 1) < SWA_HEAD
    r = lax.broadcasted_iota(jnp.int32, (BLOCK, 2 * BLOCK), 0)
    c = lax.broadcasted_iota(jnp.int32, (BLOCK, 2 * BLOCK), 1)
    valid = (c > r) & (c <= r + BLOCK) & ((c >= BLOCK) | (n > 0))
    pairs_per_kv = SWA_Q_HEADS // SWA_KV_HEADS // 2

    def block_diag(prev, cur):
        x = jnp.concatenate([prev, cur], axis=0)
        zero = jnp.zeros_like(x)
        return jnp.concatenate([jnp.where(head0, x, zero), jnp.where(head0, zero, x)], axis=0)

    kv_heads = range(SWA_KV_HEADS)
    lanes = [slice(h * LANES, (h + 1) * LANES) for h in kv_heads]
    kblks = [block_diag(kp_ref[:, hs], kc_ref[:, hs]) for hs in lanes]
    pairs = [[h * pairs_per_kv + j for j in range(pairs_per_kv)] for h in kv_heads]
    cols = [[slice(pair * LANES, (pair + 1) * LANES) for pair in ps_] for ps_ in pairs]
    ss = [[lax.dot_general(q_ref[:, cs_], kblks[h], _NT, preferred_element_type=F32) for cs_ in cols[h]]
          for h in kv_heads]
    yield

    def softmax(h):
        heads = [(i, e) for i in range(pairs_per_kv) for e in range(2)]
        tiles = [jnp.where(valid, ss[h][i][:, e * 2 * BLOCK:(e + 1) * 2 * BLOCK], -jnp.inf)
                 for i, e in heads]
        sinks = [sink_ref[2 * pairs[h][i] + e] for i, e in heads]
        ms = [jnp.maximum(jnp.max(t, axis=-1, keepdims=True), sk) for t, sk in zip(tiles, sinks)]
        exs = [jnp.exp(t - m) for t, m in zip(tiles, ms)]
        dens = [jnp.sum(ex, axis=-1, keepdims=True) + jnp.exp(sk - m)
                for ex, sk, m in zip(exs, sinks, ms)]
        return exs, dens

    def attend(h, exs, dens):
        vblk = block_diag(vp_ref[:, lanes[h]], vc_ref[:, lanes[h]])
        outs = [jnp.dot(jnp.concatenate([exs[2 * i].astype(BF16), exs[2 * i + 1].astype(BF16)], axis=1),
                        vblk, preferred_element_type=F32) for i in range(pairs_per_kv)]
        for i, cs_ in enumerate(cols[h]):
            rden = jnp.where(head0_q, 1.0 / dens[2 * i], 1.0 / dens[2 * i + 1])
            o_ref[:, cs_] = (outs[i] * rden * g_ref[:, cs_].astype(F32)).astype(o_ref.dtype)

    probs = softmax(0)
    for h in kv_heads:
        yield
        attend(h, *probs)
        if h + 1 < SWA_KV_HEADS:
            probs = softmax(h + 1)


def _proj_swa_kernel(h_ref, w_ref, sink_ref, q_ref, kp_ref, kc_ref, vp_ref, vc_ref, g_ref,
                     xq_ref, xkv_ref, xg_ref, o_ref, oswa_ref, oxa_ref, wb_ref,
                     *, act, n_split, blocks_per_seq):
    @pl.when(pl.program_id(1) == 0)
    def _():
        wb_ref[...] = w_ref[0].astype(BF16)

    step = pl.program_id(0) * pl.num_programs(1) + pl.program_id(1)
    riders = [_swa_stages(step % blocks_per_seq, sink_ref, q_ref, kp_ref, kc_ref, vp_ref, vc_ref,
                          g_ref, oswa_ref),
              _xattn_stages(xq_ref, xkv_ref, xg_ref, oxa_ref)]
    cw = wb_ref.shape[1] // n_split
    for r in riders:
        next(r)
    for s in range(n_split):
        cs_ = slice(s * cw, (s + 1) * cw)
        acc = jnp.dot(h_ref[...], wb_ref[:, cs_], preferred_element_type=F32)
        if act == "sigmoid":
            acc = _sigmoid(acc)
        o_ref[:, cs_] = acc.astype(o_ref.dtype)
        for r in riders:
            next(r, None)
    for r in riders:
        for _ in r:
            pass


def _proj_swa(h, w, layer, col_offs, tn, act, plain, q_col, kv, k_col, v_col, gates, gate_col, sinks,
              xq, xkv, xgate_col, seq_len, tm):
    m, k = h.shape
    w_xa = XA_WIDTH
    mem_len = xkv.shape[0] // (m // seq_len)
    tm = min(tm, m)
    ncol, nrow = len(col_offs), m // tm
    assert ncol * nrow * BLOCK == m, "one attention block per projection grid step"
    w_swa = SWA_WIDTH
    nb = seq_len // BLOCK
    kvw = 2 * SWA_KV_HEADS * SWA_HEAD
    step = lambda j, i: j * nrow + i
    cur = lambda col, width: (lambda j, i: (step(j, i), col // width))
    prev = lambda col, width: (
        lambda j, i: (jnp.where(step(j, i) % nb == 0, step(j, i), step(j, i) - 1), col // width))
    return pl.pallas_call(
        functools.partial(_proj_swa_kernel, act=act, n_split=tn // (2 * LANES), blocks_per_seq=nb),
        out_shape=[jax.ShapeDtypeStruct((m, ncol * tn), BF16), jax.ShapeDtypeStruct((m, w_swa), BF16),
                   jax.ShapeDtypeStruct((m, w_xa), BF16)],
        grid=(ncol, nrow),
        in_specs=[pl.BlockSpec((tm, k), lambda j, i: (i, 0)), _w_spec(layer, k, tn, col_offs),
                  pl.BlockSpec(memory_space=pltpu.SMEM),
                  pl.BlockSpec((BLOCK, w_swa), cur(q_col, w_swa)),
                  pl.BlockSpec((BLOCK, kvw), prev(k_col, kvw)), pl.BlockSpec((BLOCK, kvw), cur(k_col, kvw)),
                  pl.BlockSpec((BLOCK, kvw), prev(v_col, kvw)), pl.BlockSpec((BLOCK, kvw), cur(v_col, kvw)),
                  pl.BlockSpec((BLOCK, w_swa), cur(gate_col, w_swa)),
                  pl.BlockSpec((BLOCK, w_xa), cur(0, w_xa)),
                  pl.BlockSpec((mem_len, xkv.shape[1]), lambda j, i: (step(j, i) // nb, 0)),
                  pl.BlockSpec((BLOCK, w_xa), cur(xgate_col, w_xa))],
        out_specs=[pl.BlockSpec((tm, tn), lambda j, i: (i, j)),
                   pl.BlockSpec((BLOCK, w_swa), cur(0, w_swa)),
                   pl.BlockSpec((BLOCK, w_xa), cur(0, w_xa))],
        scratch_shapes=[pltpu.VMEM((k, tn), BF16)],
        compiler_params=_params("arbitrary", "arbitrary"),
        name="proj_merge_attn",
    )(h, w, sinks, plain, kv, kv, kv, kv, gates, xq, xkv, gates)


def _xattn_stages(q_ref, kv_ref, g_ref, o_ref):
    cols = [slice(h * XA_HEAD, (h + 1) * XA_HEAD) for h in range(XA_HEADS)]
    ss = [lax.dot_general(q_ref[:, cs_], kv_ref[:, cs_], _NT, preferred_element_type=F32) for cs_ in cols]
    yield
    ms = [jnp.max(s, axis=-1, keepdims=True) for s in ss]
    exs = [jnp.exp(s - m) for s, m in zip(ss, ms)]
    dens = [jnp.sum(ex, axis=-1, keepdims=True) for ex in exs]
    outs = [jnp.dot(ex.astype(BF16), kv_ref[:, XA_WIDTH + cs_.start:XA_WIDTH + cs_.stop],
                    preferred_element_type=F32) for ex, cs_ in zip(exs, cols)]
    yield
    for cs_, o, den in zip(cols, outs, dens):
        o_ref[:, cs_] = (o * (1.0 / den) * g_ref[:, cs_].astype(F32)).astype(o_ref.dtype)


def _merge_kernel(ya_ref, yb_ref, yc_ref, ga_ref, gb_ref, gc_ref, wa_ref, wb_ref, wc_ref, o_ref,
                  wab_ref, wbb_ref, wcb_ref):
    @pl.when(pl.program_id(1) == 0)
    def _():
        wab_ref[...] = wa_ref[0].astype(BF16)
        wbb_ref[...] = wb_ref[0].astype(BF16)
        wcb_ref[...] = wc_ref[0].astype(BF16)

    d = lambda y, w: jnp.dot(y[...], w[...], preferred_element_type=F32)
    acc = ga_ref[...].astype(F32) * d(ya_ref, wab_ref)
    acc += gb_ref[...].astype(F32) * d(yb_ref, wbb_ref)
    acc += gc_ref[...].astype(F32) * d(yc_ref, wcb_ref)
    o_ref[...] = acc.astype(o_ref.dtype)


def _merge(ya, yb, yc, gates, wa, wb, wc, layer, tm, tn):
    n = ya.shape[0]
    d = wa.shape[2]
    tm = min(tm, n)
    nj = d // tn
    y_spec = lambda a: pl.BlockSpec((tm, a.shape[1]), lambda j, i: (i, 0))
    w_spec = lambda a: pl.BlockSpec((1, a.shape[1], tn), lambda j, i: (layer, 0, j),
                                    pipeline_mode=pl.Buffered(1))
    g_spec = lambda br: pl.BlockSpec((tm, tn), lambda j, i: (i, br * nj + j))
    return pl.pallas_call(
        _merge_kernel,
        out_shape=jax.ShapeDtypeStruct((n, d), BF16),
        grid=(nj, n // tm),
        in_specs=[y_spec(ya), y_spec(yb), y_spec(yc), g_spec(0), g_spec(1), g_spec(2),
                  w_spec(wa), w_spec(wb), w_spec(wc)],
        out_specs=pl.BlockSpec((tm, tn), lambda j, i: (i, j)),
        scratch_shapes=[pltpu.VMEM((a.shape[1], tn), BF16) for a in (wa, wb, wc)],
        compiler_params=_params("arbitrary", "arbitrary"),
        name="merge",
    )(ya, yb, yc, gates, gates, gates, wa, wb, wc)


def _out_kernel(m_ref, w_ref, x_ref, g_ref, *rest):
    gn_ref, o_ref, hn_ref, wb_ref = rest if len(rest) == 4 else (None, rest[0], None, rest[1])

    @pl.when(pl.program_id(0) == 0)
    def _():
        wb_ref[...] = w_ref[0].astype(BF16)

    o = jnp.dot(m_ref[...], wb_ref[...], preferred_element_type=F32)
    ms = jnp.mean(o * o, axis=-1, keepdims=True)
    xn = x_ref[...] + o * lax.rsqrt(ms + NORM_EPS) * g_ref[...]
    o_ref[...] = xn
    if hn_ref is not None:
        ms_n = jnp.mean(xn * xn, axis=-1, keepdims=True)
        hn_ref[...] = (xn * lax.rsqrt(ms_n + NORM_EPS) * gn_ref[...]).astype(hn_ref.dtype)


def _out_proj(merged, w_out, layer, x2d, g_post, g_next, tm):
    n, d = x2d.shape
    tm = min(tm, n)
    tok = pl.BlockSpec((tm, d), lambda i: (i, 0))
    vec = pl.BlockSpec((1, d), lambda i: (0, 0))
    nxt = g_next is not None
    out = pl.pallas_call(
        _out_kernel,
        out_shape=[jax.ShapeDtypeStruct((n, d), F32)] + [jax.ShapeDtypeStruct((n, d), BF16)] * nxt,
        grid=(n // tm,),
        in_specs=[tok, pl.BlockSpec((1, d, d), lambda i: (layer, 0, 0), pipeline_mode=pl.Buffered(1)),
                  tok, vec] + [vec] * nxt,
        out_specs=[tok] + [tok] * nxt,
        scratch_shapes=[pltpu.VMEM((d, d), BF16)],
        compiler_params=_params("arbitrary"),
        name="out_proj",
    )(merged, w_out, x2d, g_post.reshape(1, d), *([g_next.reshape(1, d)] if nxt else []))
    return (out[0], out[1]) if nxt else (out[0], None)


def kernel(x, mem, g_pre, w_in, mu_shift, decay_base, decay_up, iclr_base, iclr_up, k_k, k_a, r_k,
           gn_w, gn_b, attn_sinks, g_mem, w_mem_kv, w_up_rwkv, w_up_swa, w_up_xattn, w_out, g_post):
    batch, seq_len, d = x.shape
    n = batch * seq_len
    x2d = x.reshape(n, d)
    mem2d = mem.reshape(batch * mem.shape[1], d)
    w = RWKV_WIDTH
    kvh = SWA_KV_HEADS * SWA_HEAD
    o_gate_a = SHIFT_COLS
    o_q_b = o_gate_a + w
    o_kv_b = o_q_b + SWA_WIDTH
    o_gate_b = o_kv_b + 2 * kvh
    o_q_c = o_gate_b + SWA_WIDTH
    o_gate_c = o_q_c + XA_WIDTH
    o_merge = o_gate_c + XA_WIDTH

    tn = 1024
    blocks = lambda off, width: [off + t * tn for t in range(width // tn)]
    for l in range(DEPTH):
        wda = jnp.zeros((LANES, 2 * w), F32)
        wda = wda.at[:DECAY_RANK, :w].set(decay_up[l]).at[DECAY_RANK:, w:].set(iclr_up[l])

        if l == 0:
            h = _rmsnorm(x2d, g_pre[l], 512)
        p_rkv = _proj(h, w_in, l, blocks(0, 3 * w), tn, out_dtype=F32, tm=2048, name="proj_rkv")
        p_qc, p_kv, p_da = _proj_small(h, w_in, l, o_q_c, XA_HEAD ** -0.5, o_kv_b, 3 * w, 1024)
        p_gq, prep = _proj_prep(h, w_in, l, [o_gate_a, o_gate_b, o_gate_c, o_q_b], tn,
                                [1.0, 1.0, 1.0, SWA_HEAD ** -0.5], 3, p_rkv, p_da, seq_len, mu_shift[l], wda,
                                decay_base[l], iclr_base[l], k_k[l], k_a[l], r_k[l].reshape(-1), 1024)
        mem_n = _rmsnorm(mem2d, g_mem[l], 512)
        kv_c = _proj(mem_n, w_mem_kv, l, blocks(0, 2 * XA_WIDTH), tn, out_dtype=BF16, tm=1024,
                     name="mem_kv")
        tm_m = min(1024, n)
        ncol_m = (n // BLOCK) // (n // tm_m)
        tn_m = N_BRANCH * d // ncol_m
        p_merge, y_b, y_c = _proj_swa(h, w_in, l, [o_merge + t * tn_m for t in range(ncol_m)], tn_m,
                                      "sigmoid", p_gq, 3 * w, p_kv, 0, 2 * kvh, p_gq, w, attn_sinks[l],
                                      p_qc, kv_c, w + SWA_WIDTH, seq_len, tm_m)

        y_a = _rwkv_chunk(prep, p_gq, 0, gn_w[l], gn_b[l], batch, seq_len)

        merged = _merge(y_a, y_b, y_c, p_merge, w_up_rwkv, w_up_swa, w_up_xattn, l, 1024, 1024)
        x2d, h = _out_proj(merged, w_out, l, x2d, g_post[l], g_pre[l + 1] if l + 1 < DEPTH else None, 512)
    return x2d.reshape(batch, seq_len, d)
```

```python
import functools

import jax
import jax.numpy as jnp
import numpy as np
from jax import lax
from jax.experimental import pallas as pl
from jax.experimental.pallas import tpu as pltpu

F32 = jnp.float32
BF16 = jnp.bfloat16

D_MODEL = 2048
DEPTH = 2
MEM_LEN = 256
NORM_EPS = 1e-6
RWKV_WIDTH = 1024
RWKV_HEAD = 64
DECAY_RANK = 64
ICLR_RANK = 64
GN_EPS = 64e-5
SWA_HEAD = 64
SWA_Q_HEADS = 16
SWA_KV_HEADS = 2
SWA_WIDTH = SWA_Q_HEADS * SWA_HEAD
BLOCK = 128
XA_HEADS = 4
XA_HEAD = 256
XA_WIDTH = XA_HEADS * XA_HEAD
N_BRANCH = 3
SHIFT_COLS = 3 * RWKV_WIDTH + DECAY_RANK + ICLR_RANK

LANES = 128
CHUNK = 64
PAIR = 2 * RWKV_HEAD
VMEM_LIMIT = 56 * 1024 * 1024
LHS_RING = 3

_NT = (((1,), (1,)), ((), ()))
_TN = (((0,), (0,)), ((), ()))


def _params(*sem):
    return pltpu.CompilerParams(dimension_semantics=sem, vmem_limit_bytes=VMEM_LIMIT)


def _sigmoid(x):
    return 1.0 / (1.0 + jnp.exp(-x))


def _split3(x):
    hi = x.astype(BF16)
    r1 = x - hi.astype(F32)
    mid = r1.astype(BF16)
    lo = (r1 - mid.astype(F32)).astype(BF16)
    return hi, mid, lo


def _dot_exact_lhs(m, x):
    hi, mid, lo = _split3(x)
    d = lambda a: jnp.dot(m, a, preferred_element_type=F32)
    return d(hi) + d(mid) + d(lo)


def _rmsnorm_kernel(x_ref, g_ref, o_ref):
    xf = x_ref[...]
    ms = jnp.mean(xf * xf, axis=-1, keepdims=True)
    o_ref[...] = (xf * lax.rsqrt(ms + NORM_EPS) * g_ref[...]).astype(o_ref.dtype)


def _rmsnorm(x2d, g, tm):
    m, d = x2d.shape
    tm = min(tm, m)
    return pl.pallas_call(
        _rmsnorm_kernel,
        out_shape=jax.ShapeDtypeStruct((m, d), BF16),
        grid=(m // tm,),
        in_specs=[pl.BlockSpec((tm, d), lambda i: (i, 0)),
                  pl.BlockSpec((1, d), lambda i: (0, 0))],
        out_specs=pl.BlockSpec((tm, d), lambda i: (i, 0)),
        compiler_params=_params("parallel"),
        name="rmsnorm",
    )(x2d, g.reshape(1, d))


def _pick(j, values):
    out = values[0]
    for idx in range(1, len(values)):
        out = jnp.where(j == idx, values[idx], out)
    return out


def _w_spec(layer, k, tn, col_offs, single=False):
    assert all(off % LANES == 0 for off in col_offs)
    mode = dict(pipeline_mode=pl.Buffered(1)) if single else {}
    if len(col_offs) == 1:
        return pl.BlockSpec((pl.Element(1), pl.Element(k), pl.Element(tn)),
                            lambda j, i: (layer, 0, col_offs[0]), **mode)
    return pl.BlockSpec((pl.Element(1), pl.Element(k), pl.Element(tn)),
                        lambda j, i: (layer, 0, pl.multiple_of(_pick(j, col_offs), LANES)), **mode)


def _proj_kernel(h_ref, w_ref, o_ref, wb_ref):
    @pl.when(pl.program_id(1) == 0)
    def _():
        wb_ref[...] = w_ref[0].astype(BF16)

    o_ref[...] = jnp.dot(h_ref[...], wb_ref[...], preferred_element_type=F32).astype(o_ref.dtype)


def _proj(h, w, layer, col_offs, tn, *, out_dtype, tm, name):
    m, k = h.shape
    tm = min(tm, m)
    nb = len(col_offs)
    return pl.pallas_call(
        _proj_kernel,
        out_shape=jax.ShapeDtypeStruct((m, nb * tn), out_dtype),
        grid=(nb, m // tm),
        in_specs=[pl.BlockSpec((tm, k), lambda j, i: (i, 0)), _w_spec(layer, k, tn, col_offs)],
        out_specs=pl.BlockSpec((tm, tn), lambda j, i: (i, j)),
        scratch_shapes=[pltpu.VMEM((k, tn), BF16)],
        compiler_params=_params("arbitrary", "arbitrary"),
        name=name,
    )(h, w)


def _proj_small_kernel(h_ref, wq_ref, wkv_ref, wz_ref, oq_ref, okv_ref, oz_ref, wqb_ref, wkvb_ref, wzb_ref,
                       *, q_scale):
    @pl.when(pl.program_id(0) == 0)
    def _():
        wqb_ref[...] = (wq_ref[0] * q_scale).astype(BF16)
        wzb_ref[...] = wz_ref[0].astype(BF16)
        wf = wkv_ref[0]
        head0 = lax.broadcasted_iota(jnp.int32, (wf.shape[0], LANES), 1) < SWA_HEAD
        for g in range(wf.shape[1] // LANES):
            a = wf[:, g * LANES:(g + 1) * LANES]
            ra = pltpu.roll(a, SWA_HEAD, 1)
            wkvb_ref[:, 2 * g * LANES:(2 * g + 1) * LANES] = jnp.where(head0, a, ra).astype(BF16)
            wkvb_ref[:, (2 * g + 1) * LANES:(2 * g + 2) * LANES] = jnp.where(head0, ra, a).astype(BF16)

    h = h_ref[...]
    oq_ref[...] = jnp.dot(h, wqb_ref[...], preferred_element_type=F32).astype(oq_ref.dtype)
    okv_ref[...] = jnp.dot(h, wkvb_ref[...], preferred_element_type=F32).astype(okv_ref.dtype)
    oz_ref[...] = jnp.dot(h, wzb_ref[...], preferred_element_type=F32)


def _proj_small(h, w, layer, q_off, q_scale, kv_off, z_off, tm):
    m, k = h.shape
    tm = min(tm, m)
    wq, wkv, wz = XA_WIDTH, 2 * SWA_KV_HEADS * SWA_HEAD, DECAY_RANK + ICLR_RANK
    win = lambda width, off: pl.BlockSpec((pl.Element(1), pl.Element(k), pl.Element(width)),
                                          lambda i: (layer, 0, off), pipeline_mode=pl.Buffered(1))
    tok = lambda width: pl.BlockSpec((tm, width), lambda i: (i, 0))
    return pl.pallas_call(
        functools.partial(_proj_small_kernel, q_scale=q_scale),
        out_shape=[jax.ShapeDtypeStruct((m, wq), BF16), jax.ShapeDtypeStruct((m, 2 * wkv), BF16),
                   jax.ShapeDtypeStruct((m, wz), F32)],
        grid=(m // tm,),
        in_specs=[tok(k), win(wq, q_off), win(wkv, kv_off), win(wz, z_off)],
        out_specs=[tok(wq), tok(2 * wkv), tok(wz)],
        scratch_shapes=[pltpu.VMEM((k, wq), BF16), pltpu.VMEM((k, 2 * wkv), BF16), pltpu.VMEM((k, wz), BF16)],
        compiler_params=_params("arbitrary"),
        name="proj_small",
    )(h, w, w, w)


def _prep_stages(first, p_ref, prev_ref, z_ref, zprev_ref, mu_ref, wst_ref, wlo_ref, dbase_ref, ibase_ref,
                 kk_ref, ka_ref, rk_ref, tri_ref, ones_ref,
                 at_ref, rt_ref, bt_ref, kt_ref, bd_ref, kd_ref, v_ref, bonus_ref, wc_ref):
    tp = p_ref.shape[0]
    w = RWKV_WIDTH
    gw = 2 * LANES
    groups = [slice(g * gw, (g + 1) * gw) for g in range(w // gw)]

    def shifted(c0, width, cur_ref=p_ref, before_ref=prev_ref, mu0=0):
        cols = slice(c0, c0 + width)
        x = cur_ref[:, cols]
        xs = pltpu.roll(x, 1, 0)
        prev_last = jnp.where(first, 0.0, before_ref[7:8, cols])
        row = lax.broadcasted_iota(jnp.int32, (8, width), 0)
        xprev = jnp.concatenate([jnp.where(row == 0, prev_last, xs[:8]), xs[8:]], axis=0)
        return x + mu_ref[:, mu0 + c0:mu0 + c0 + width] * (xprev - x)

    def dot(a, b):
        return jnp.dot(a, b, preferred_element_type=F32)

    def split2(x):
        hi = x.astype(BF16)
        return hi, (x - hi.astype(F32)).astype(BF16)

    def head_sum(t):
        hi, lo = split2(t)
        return dot(hi, ones_ref[...]) + dot(lo, ones_ref[...])

    z = shifted(0, LANES, z_ref, zprev_ref, 3 * w)
    lane = lax.broadcasted_iota(jnp.int32, z.shape, 1)
    z = jnp.where(lane < DECAY_RANK, jnp.tanh(z), z)
    z_hi, z_lo = split2(z)
    zz = jnp.concatenate([z_hi, z_lo], axis=1)
    up = lambda c0: dot(zz, wst_ref[:, c0:c0 + gw]) + dot(z_hi, wlo_ref[:, c0:c0 + gw])
    decs = [up(cols.start) for cols in groups]
    apres = [up(w + cols.start) for cols in groups]
    yield

    a_s, logws = [], []
    for cols, dec, apre in zip(groups, decs, apres):
        nz = -(dbase_ref[:, cols] + dec)
        a_s.append(_sigmoid(ibase_ref[:, cols] + apre))
        softplus = jnp.maximum(nz, 0.0) + jnp.log(1.0 + jnp.exp(-jnp.abs(nz)))
        logws.append(-jnp.exp(-softplus - 0.5))
    css = [_dot_exact_lhs(tri_ref[...], logw) for logw in logws]
    yield

    rs = [shifted(cols.start, gw) for cols in groups]
    ks = [shifted(w + cols.start, gw) for cols in groups]
    vs = [shifted(2 * w + cols.start, gw) for cols in groups]
    kks = [k * kk_ref[:, cols] for k, cols in zip(ks, groups)]
    kps = [k * (1.0 + (a - 1.0) * ka_ref[:, cols]) for k, a, cols in zip(ks, a_s, groups)]
    norms = [head_sum(kk * kk) for kk in kks]
    coefs = [head_sum(r * kp * rk_ref[:, cols]) for r, kp, cols in zip(rs, kps, groups)]
    yield

    for g, cols in enumerate(groups):
        cs, logw, a, kp, v = css[g], logws[g], a_s[g], kps[g], vs[g]
        lasts = [cs[c * CHUNK + CHUNK - 1:c * CHUNK + CHUNK, :] for c in range(tp // CHUNK)]
        cs_last = jnp.concatenate([jnp.broadcast_to(t, (CHUNK, gw)) for t in lasts], axis=0)
        kk = kks[g] * jnp.minimum(lax.rsqrt(norms[g]), 1e12)
        b = kk * a
        e_neg = jnp.exp(-cs)
        e_d = jnp.exp(cs_last - cs)
        at_ref[:, cols] = (-kk * jnp.exp(cs - logw)).astype(BF16)
        rt_ref[:, cols] = (rs[g] * jnp.exp(cs)).astype(BF16)
        bt_ref[:, cols] = (b * e_neg).astype(BF16)
        kt_ref[:, cols] = (kp * e_neg).astype(BF16)
        bd_ref[:, cols] = (b * e_d).astype(BF16)
        kd_ref[:, cols] = (kp * e_d).astype(BF16)
        v_ref[:, cols] = v.astype(BF16)
        bonus_ref[:, cols] = coefs[g] * v
        for c, t in enumerate(lasts):
            wc_ref[0, c:c + 1, cols] = jnp.exp(t)


N_PREP_IN, N_PREP_OUT = 14, 9


def _proj_prep_kernel(h_ref, w_ref, *refs, scales, n_silu, n_split, blocks_per_seq):
    prep_in = refs[:N_PREP_IN]
    o_ref = refs[N_PREP_IN]
    prep_out = refs[N_PREP_IN + 1:N_PREP_IN + 1 + N_PREP_OUT]
    wb_ref, hbuf_ref, hsem = refs[-3:]
    j = pl.program_id(0)
    nrow = pl.num_programs(1)
    nsteps = pl.num_programs(0) * nrow
    step = j * nrow + pl.program_id(1)
    tm = hbuf_ref.shape[1]

    def h_copy(s):
        slot = s % LHS_RING
        rows = pl.ds(pl.multiple_of((s % nrow) * tm, tm), tm)
        return pltpu.make_async_copy(h_ref.at[rows, :], hbuf_ref.at[slot], hsem.at[slot])

    @pl.when(step == 0)
    def _():
        for s in range(LHS_RING - 1):
            @pl.when(s < nsteps)
            def _():
                h_copy(s).start()

    @pl.when(step + LHS_RING - 1 < nsteps)
    def _():
        h_copy(step + LHS_RING - 1).start()

    @pl.when(pl.program_id(1) == 0)
    def _():
        wb_ref[...] = (w_ref[0] * _pick(j, scales)).astype(BF16)

    h_copy(step).wait()
    h_now = hbuf_ref.at[step % LHS_RING]
    stages = _prep_stages(step % blocks_per_seq == 0, *prep_in, *prep_out)
    cw = wb_ref.shape[1] // n_split
    next(stages)
    for s in range(n_split):
        cs_ = slice(s * cw, (s + 1) * cw)
        acc = jnp.dot(h_now[...], wb_ref[:, cs_], preferred_element_type=F32)
        o_ref[:, cs_] = jnp.where(j < n_silu, acc * _sigmoid(acc), acc).astype(o_ref.dtype)
        next(stages, None)
    for _ in stages:
        pass


def _proj_prep(h, w_in, layer, col_offs, tn, scales, n_silu, p, pz, seq_len, mu, wda, dbase, ibase,
               k_k, k_a, r_k, tm):
    n, k = h.shape
    w = RWKV_WIDTH
    tm = min(tm, n)
    ncol, nrow = len(col_offs), n // tm
    tp = n // (ncol * nrow)
    assert tp % CHUNK == 0 and seq_len % tp == 0 and tp % 8 == 0
    cpb = tp // CHUNK
    tri = np.tril(np.ones((CHUNK, CHUNK), np.float32))
    tri = jnp.asarray(np.kron(np.eye(cpb, dtype=np.float32), tri), BF16)
    ones_bd = jnp.asarray(np.kron(np.eye(2 * LANES // RWKV_HEAD, dtype=np.float32),
                                  np.ones((RWKV_HEAD, RWKV_HEAD), np.float32)), BF16)
    w_hi = wda.astype(BF16)
    w_lo = (wda - w_hi.astype(F32)).astype(BF16)
    w_st = jnp.concatenate([w_hi, w_hi], axis=0)
    row = lambda a: a.reshape(1, -1)
    full = lambda shape: pl.BlockSpec(shape, lambda j, i: (0,) * len(shape))
    step = lambda j, i: j * nrow + i
    tok = lambda width: pl.BlockSpec((tp, width), lambda j, i: (step(j, i), 0))
    before = lambda width: pl.BlockSpec(
        (8, width), lambda j, i: (jnp.maximum(step(j, i) * (tp // 8) - 1, 0), 0))
    outs = pl.pallas_call(
        functools.partial(_proj_prep_kernel, scales=scales, n_silu=n_silu, n_split=tn // (2 * LANES),
                          blocks_per_seq=seq_len // tp),
        out_shape=[jax.ShapeDtypeStruct((n, ncol * tn), BF16)]
        + [jax.ShapeDtypeStruct((n, w), BF16)] * 7
        + [jax.ShapeDtypeStruct((n, w), F32), jax.ShapeDtypeStruct((n // tp, cpb, w), F32)],
        grid=(ncol, nrow),
        in_specs=[pl.BlockSpec(memory_space=pl.ANY), _w_spec(layer, k, tn, col_offs, single=True),
                  tok(3 * w), before(3 * w), tok(LANES), before(LANES),
                  full((1, SHIFT_COLS)), full((2 * LANES, 2 * w)), full((LANES, 2 * w)),
                  full((1, w)), full((1, w)), full((1, w)), full((1, w)), full((1, w)),
                  full((tp, tp)), full((2 * LANES, 2 * LANES))],
        out_specs=[pl.BlockSpec((tm, tn), lambda j, i: (i, j))] + [tok(w)] * 8
        + [pl.BlockSpec((1, cpb, w), lambda j, i: (step(j, i), 0, 0))],
        scratch_shapes=[pltpu.VMEM((k, tn), BF16), pltpu.VMEM((LHS_RING, tm, k), BF16),
                        pltpu.SemaphoreType.DMA((LHS_RING,))],
        compiler_params=_params("arbitrary", "arbitrary"),
        name="proj_gq_prep",
    )(h, w_in, p, p, pz, pz, row(mu), w_st, w_lo, row(dbase), row(ibase), row(k_k), row(k_a), row(r_k),
      tri, ones_bd)
    return outs[0], outs[1:]


def _rwkv_chunk_kernel(at_ref, rt_ref, bt_ref, kt_ref, bd_ref, kd_ref, v_ref, bonus_ref, wc_ref,
                       gate_ref, gnw_ref, gnb_ref, avg_ref, o_ref,
                       pt_ref, qt_ref, g_ref, y0_ref, y_ref, *, n_chunks, n_pairs):
    c2 = 2 * CHUNK
    lane = lax.broadcasted_iota(jnp.int32, (CHUNK, PAIR), 1)
    head0 = lane < RWKV_HEAD
    ri = lax.broadcasted_iota(jnp.int32, (2 * c2, 2 * c2), 0)
    ci = lax.broadcasted_iota(jnp.int32, (2 * c2, 2 * c2), 1)
    keep = (ci & (CHUNK - 1)) < (ri & (CHUNK - 1)) + jnp.where(ri < c2, 0, 1)
    eye = (lax.broadcasted_iota(jnp.int32, (c2, c2), 0)
           == lax.broadcasted_iota(jnp.int32, (c2, c2), 1)).astype(F32)
    zeros_b = jnp.zeros((c2, c2), BF16)

    def stacked(ref, sl, ls):
        x = ref[sl, ls]
        zero = jnp.zeros_like(x)
        return jnp.concatenate([jnp.where(head0, x, zero), jnp.where(head0, zero, x)], axis=0)

    def dot(a, b):
        return jnp.dot(a, b, preferred_element_type=F32)

    def phase1_stages(items):
        def ld(ref, it):
            p, c = it
            return stacked(ref, pl.ds(pl.multiple_of(c * CHUNK, CHUNK), CHUNK),
                           slice(p * PAIR, (p + 1) * PAIR))

        lms = [lax.dot_general(jnp.concatenate([ld(at_ref, it), ld(rt_ref, it)], axis=0),
                               jnp.concatenate([ld(bt_ref, it), ld(kt_ref, it)], axis=0), _NT,
                               preferred_element_type=F32) for it in items]
        lms = [jnp.where(keep, lm, 0.0) for lm in lms]
        tops = [lm[:c2, :].astype(BF16) for lm in lms]
        m_rs = [lm[c2:, :].astype(BF16) for lm in lms]
        yield
        invs = [eye + lm[:c2, :c2] for lm in lms]
        firsts = [dot(top, jnp.concatenate([jnp.concatenate([top[:, :c2], zeros_b], axis=1),
                                            jnp.concatenate([zeros_b, ld(v_ref, it)], axis=1)], axis=0))
                  for top, it in zip(tops, items)]
        lps = [f[:, :c2] for f in firsts]
        lak_vs = [f[:, c2:].astype(BF16) for f in firsts]
        yield
        n_fac = int(np.log2(CHUNK)) - 1
        for f in range(n_fac - 1):
            lp_bs = [lp.astype(BF16) for lp in lps]
            prs = [dot(lp_b, jnp.concatenate([lp_b, inv.astype(BF16)], axis=1))
                   for lp_b, inv in zip(lp_bs, invs)]
            lps = [pr[:, :c2] for pr in prs]
            invs = [inv + pr[:, c2:] for inv, pr in zip(invs, prs)]
            yield
        invs = [inv + dot(lp.astype(BF16), inv.astype(BF16)) for lp, inv in zip(lps, invs)]
        yield
        xu_bs = [dot(inv.astype(BF16), jnp.concatenate([ld(at_ref, it), lak_v], axis=1)).astype(BF16)
                 for inv, lak_v, it in zip(invs, lak_vs, items)]
        rhss = [jnp.concatenate([xu_b, jnp.concatenate([zeros_b, ld(v_ref, it)], axis=1)], axis=0)
                for xu_b, it in zip(xu_bs, items)]
        yield
        gys = [dot(m_r, rhs) for m_r, rhs in zip(m_rs, rhss)]
        yield
        pqs = [lax.dot_general(rhs, jnp.concatenate([ld(bd_ref, it), ld(kd_ref, it)], axis=0), _TN,
                               preferred_element_type=F32)
               for rhs, it in zip(rhss, items)]
        for (p, c), gy, pq in zip(items, gys, pqs):
            g_ref[p, c] = (ld(rt_ref, (p, c)).astype(F32) + gy[:, :c2]).astype(BF16)
            y0_ref[p, c] = gy[:, c2:]
            pt_ref[p, c] = pq[:c2, :].astype(BF16)
            qt_ref[p, c] = pq[c2:, :]

    def phase2_chunk(c, sts):
        sl = pl.ds(pl.multiple_of(c * CHUNK, CHUNK), CHUNK)
        new = []
        for p in range(n_pairs):
            ls = slice(p * PAIR, (p + 1) * PAIR)
            st = sts[p]
            st_b = st.astype(BF16)
            y_d = lax.dot_general(g_ref[p, c], st_b, _NT, preferred_element_type=F32) + y0_ref[p, c]
            y_ref[sl, ls] = y_d[:CHUNK, :] + y_d[CHUNK:, :]
            new.append(st * wc_ref[0, c, :, ls] + dot(st_b, pt_ref[p, c]) + qt_ref[p, c])
        return tuple(new)

    group = 4
    n_groups = n_chunks // group

    def run_group(i, sts, build, scan):
        stages = (phase1_stages([(p, i * group + u) for p in range(n_pairs) for u in range(group)])
                  if build else iter(()))
        todo = [(i - 1) * group + u for u in range(group)] if scan else []
        for s, _ in enumerate(stages):
            if todo and s % 2 == 1:
                sts = phase2_chunk(todo.pop(0), sts)
        for c in todo:
            sts = phase2_chunk(c, sts)
        return sts

    sts = tuple(jnp.zeros((c2, c2), F32) for _ in range(n_pairs))
    sts = run_group(0, sts, True, False)
    sts = lax.fori_loop(1, n_groups, lambda i, s: run_group(i, s, True, True), sts)

    rows3 = 4 * CHUNK
    head0_r = lax.broadcasted_iota(jnp.int32, (rows3, PAIR), 1) < RWKV_HEAD

    def head_mean(t):
        s0 = jnp.sum(jnp.where(head0_r, t, 0.0), axis=-1, keepdims=True)
        s1 = jnp.sum(jnp.where(head0_r, 0.0, t), axis=-1, keepdims=True)
        return jnp.where(head0_r, s0, s1) * (1.0 / RWKV_HEAD)

    def head_mean_mxu(t):
        hi = t.astype(BF16)
        lo = (t - hi.astype(F32)).astype(BF16)
        return dot(hi, avg_ref[...]) + dot(lo, avg_ref[...])

    unroll3 = 2
    trip_rows = rows3 * unroll3

    def phase3(t):
        items = [(pl.ds(t * trip_rows + u * rows3, rows3), slice(p * PAIR, (p + 1) * PAIR))
                 for u in range(unroll3) for p in range(n_pairs)]
        ys = [y_ref[sl, ls] for sl, ls in items]
        ds = [y - head_mean(y) for y in ys]
        vs = [head_mean_mxu(d * d) for d in ds]
        for (sl, ls), d, var in zip(items, ds, vs):
            yn = d * lax.rsqrt(var + GN_EPS) * gnw_ref[:, ls] + gnb_ref[:, ls]
            o_ref[sl, ls] = ((yn + bonus_ref[sl, ls]) * gate_ref[sl, ls].astype(F32)).astype(o_ref.dtype)

    tail = [(n_groups - 1) * group + u for u in range(group)]
    trips = list(range(n_chunks * CHUNK // trip_rows))
    while trips and (trips[0] + 1) * trip_rows <= tail[0] * CHUNK:
        if tail[1:]:
            sts = phase2_chunk(tail.pop(0), sts)
        phase3(trips.pop(0))
    for c in tail:
        sts = phase2_chunk(c, sts)
    for t in trips:
        phase3(t)


def _rwkv_chunk(prep, gates, gate_col, gn_w, gn_b, batch, seq_len):
    at, rt, bt, kt, bd, kd, v, bonus, wc = prep
    n, w = at.shape
    n_chunks = seq_len // CHUNK
    n_pairs = 2
    bw = n_pairs * PAIR
    wc = wc.reshape(batch, n_chunks, 1, w)
    avg = jnp.asarray(np.kron(np.eye(PAIR // RWKV_HEAD, dtype=np.float32),
                              np.full((RWKV_HEAD, RWKV_HEAD), 1.0 / RWKV_HEAD, np.float32)), BF16)
    tok = pl.BlockSpec((seq_len, bw), lambda b, h: (b, h))
    gate_spec = pl.BlockSpec((seq_len, bw), lambda b, h: (b, gate_col // bw + h))
    vec = pl.BlockSpec((1, bw), lambda b, h: (0, h))
    c2 = 2 * CHUNK
    ops = lambda dt: pltpu.VMEM((n_pairs, n_chunks, c2, c2), dt)
    return pl.pallas_call(
        functools.partial(_rwkv_chunk_kernel, n_chunks=n_chunks, n_pairs=n_pairs),
        out_shape=jax.ShapeDtypeStruct((n, w), BF16),
        grid=(batch, w // bw),
        in_specs=[tok] * 8 + [pl.BlockSpec((1, n_chunks, 1, bw), lambda b, h: (b, 0, 0, h)),
                              gate_spec, vec, vec,
                              pl.BlockSpec((PAIR, PAIR), lambda b, h: (0, 0))],
        out_specs=tok,
        scratch_shapes=[ops(BF16), ops(F32), ops(BF16), ops(F32), pltpu.VMEM((seq_len, bw), F32)],
        compiler_params=_params("parallel", "parallel"),
        name="rwkv_chunk",
    )(at, rt, bt, kt, bd, kd, v, bonus, wc, gates, gn_w.reshape(1, w), gn_b.reshape(1, w), avg)


def _swa_stages(n, sink_ref, q_ref, kp_ref, kc_ref, vp_ref, vc_ref, g_ref, o_ref):
    lane = lax.broadcasted_iota(jnp.int32, (2 * BLOCK, LANES), 1)
    head0 = lane < SWA_HEAD
    head0_q = lax.broadcasted_iota(jnp.int32, (BLOCK, LANES), 1) < SWA_HEAD
    r = lax.broadcasted_iota(jnp.int32, (BLOCK, 2 * BLOCK), 0)
    c = lax.broadcasted_iota(jnp.int32, (BLOCK, 2 * BLOCK), 1)
    valid = (c > r) & (c <= r + BLOCK) & ((c >= BLOCK) | (n > 0))
    pairs_per_kv = SWA_Q_HEADS // SWA_KV_HEADS // 2

    def block_diag(prev, cur):
        x = jnp.concatenate([prev, cur], axis=0)
        zero = jnp.zeros_like(x)
        return jnp.concatenate([jnp.where(head0, x, zero), jnp.where(head0, zero, x)], axis=0)

    kv_heads = range(SWA_KV_HEADS)
    lanes = [slice(h * LANES, (h + 1) * LANES) for h in kv_heads]
    kblks = [block_diag(kp_ref[:, hs], kc_ref[:, hs]) for hs in lanes]
    pairs = [[h * pairs_per_kv + j for j in range(pairs_per_kv)] for h in kv_heads]
    cols = [[slice(pair * LANES, (pair + 1) * LANES) for pair in ps_] for ps_ in pairs]
    ss = [[lax.dot_general(q_ref[:, cs_], kblks[h], _NT, preferred_element_type=F32) for cs_ in cols[h]]
          for h in kv_heads]
    yield

    def softmax(h):
        heads = [(i, e) for i in range(pairs_per_kv) for e in range(2)]
        tiles = [jnp.where(valid, ss[h][i][:, e * 2 * BLOCK:(e + 1) * 2 * BLOCK], -jnp.inf)
                 for i, e in heads]
        sinks = [sink_ref[2 * pairs[h][i] + e] for i, e in heads]
        ms = [jnp.maximum(jnp.max(t, axis=-1, keepdims=True), sk) for t, sk in zip(tiles, sinks)]
        exs = [jnp.exp(t - m) for t, m in zip(tiles, ms)]
        dens = [jnp.sum(ex, axis=-1, keepdims=True) + jnp.exp(sk - m)
                for ex, sk, m in zip(exs, sinks, ms)]
        return exs, dens

    def attend(h, exs, dens):
        vblk = block_diag(vp_ref[:, lanes[h]], vc_ref[:, lanes[h]])
        outs = [jnp.dot(jnp.concatenate([exs[2 * i].astype(BF16), exs[2 * i + 1].astype(BF16)], axis=1),
                        vblk, preferred_element_type=F32) for i in range(pairs_per_kv)]
        for i, cs_ in enumerate(cols[h]):
            rden = jnp.where(head0_q, 1.0 / dens[2 * i], 1.0 / dens[2 * i + 1])
            o_ref[:, cs_] = (outs[i] * rden * g_ref[:, cs_].astype(F32)).astype(o_ref.dtype)

    probs = softmax(0)
    for h in kv_heads:
        yield
        attend(h, *probs)
        if h + 1 < SWA_KV_HEADS:
            probs = softmax(h + 1)


def _proj_swa_kernel(h_ref, w_ref, sink_ref, q_ref, kp_ref, kc_ref, vp_ref, vc_ref, g_ref,
                     o_ref, oswa_ref, wb_ref, *, act, n_split, blocks_per_seq):
    @pl.when(pl.program_id(1) == 0)
    def _():
        wb_ref[...] = w_ref[0].astype(BF16)

    step = pl.program_id(0) * pl.num_programs(1) + pl.program_id(1)
    stages = _swa_stages(step % blocks_per_seq, sink_ref, q_ref, kp_ref, kc_ref, vp_ref, vc_ref,
                         g_ref, oswa_ref)
    cw = wb_ref.shape[1] // n_split
    next(stages)
    for s in range(n_split):
        cs_ = slice(s * cw, (s + 1) * cw)
        acc = jnp.dot(h_ref[...], wb_ref[:, cs_], preferred_element_type=F32)
        if act == "sigmoid":
            acc = _sigmoid(acc)
        o_ref[:, cs_] = acc.astype(o_ref.dtype)
        next(stages, None)
    for _ in stages:
        pass


def _proj_swa(h, w, layer, col_offs, tn, act, plain, q_col, kv, k_col, v_col, gates, gate_col, sinks,
              seq_len, tm):
    m, k = h.shape
    tm = min(tm, m)
    ncol, nrow = len(col_offs), m // tm
    assert ncol * nrow * BLOCK == m, "one attention block per projection grid step"
    w_swa = SWA_WIDTH
    nb = seq_len // BLOCK
    kvw = 2 * SWA_KV_HEADS * SWA_HEAD
    step = lambda j, i: j * nrow + i
    cur = lambda col, width: (lambda j, i: (step(j, i), col // width))
    prev = lambda col, width: (
        lambda j, i: (jnp.where(step(j, i) % nb == 0, step(j, i), step(j, i) - 1), col // width))
    return pl.pallas_call(
        functools.partial(_proj_swa_kernel, act=act, n_split=tn // (2 * LANES), blocks_per_seq=nb),
        out_shape=[jax.ShapeDtypeStruct((m, ncol * tn), BF16), jax.ShapeDtypeStruct((m, w_swa), BF16)],
        grid=(ncol, nrow),
        in_specs=[pl.BlockSpec((tm, k), lambda j, i: (i, 0)), _w_spec(layer, k, tn, col_offs),
                  pl.BlockSpec(memory_space=pltpu.SMEM),
                  pl.BlockSpec((BLOCK, w_swa), cur(q_col, w_swa)),
                  pl.BlockSpec((BLOCK, kvw), prev(k_col, kvw)), pl.BlockSpec((BLOCK, kvw), cur(k_col, kvw)),
                  pl.BlockSpec((BLOCK, kvw), prev(v_col, kvw)), pl.BlockSpec((BLOCK, kvw), cur(v_col, kvw)),
                  pl.BlockSpec((BLOCK, w_swa), cur(gate_col, w_swa))],
        out_specs=[pl.BlockSpec((tm, tn), lambda j, i: (i, j)),
                   pl.BlockSpec((BLOCK, w_swa), cur(0, w_swa))],
        scratch_shapes=[pltpu.VMEM((k, tn), BF16)],
        compiler_params=_params("arbitrary", "arbitrary"),
        name="proj_merge_swa",
    )(h, w, sinks, plain, kv, kv, kv, kv, gates)


def _xattn_kernel(q_ref, kv_ref, g_ref, o_ref):
    for h in range(XA_HEADS):
        hs = slice(h * XA_HEAD, (h + 1) * XA_HEAD)
        vs = slice(XA_WIDTH + h * XA_HEAD, XA_WIDTH + (h + 1) * XA_HEAD)
        s = lax.dot_general(q_ref[:, hs], kv_ref[:, hs], _NT, preferred_element_type=F32)
        m = jnp.max(s, axis=-1, keepdims=True)
        ex = jnp.exp(s - m)
        p = (ex / jnp.sum(ex, axis=-1, keepdims=True)).astype(BF16)
        o = jnp.dot(p, kv_ref[:, vs], preferred_element_type=F32)
        o_ref[:, hs] = (o * g_ref[:, hs].astype(F32)).astype(o_ref.dtype)


def _xattn(plain, q_col, kv, gates, gate_col, batch, seq_len, tq):
    n = plain.shape[0]
    w = XA_WIDTH
    tq = min(tq, seq_len)
    nq = seq_len // tq
    m = kv.shape[0] // batch
    tok = lambda col: pl.BlockSpec((tq, w), lambda b, i: (b * nq + i, col // w))
    return pl.pallas_call(
        _xattn_kernel,
        out_shape=jax.ShapeDtypeStruct((n, w), BF16),
        grid=(batch, nq),
        in_specs=[tok(q_col), pl.BlockSpec((m, kv.shape[1]), lambda b, i: (b, 0)), tok(gate_col)],
        out_specs=tok(0),
        compiler_params=_params("parallel", "parallel"),
        name="xattn",
    )(plain, kv, gates)


def _merge_kernel(ya_ref, yb_ref, yc_ref, ga_ref, gb_ref, gc_ref, wa_ref, wb_ref, wc_ref, o_ref,
                  wab_ref, wbb_ref, wcb_ref):
    @pl.when(pl.program_id(1) == 0)
    def _():
        wab_ref[...] = wa_ref[0].astype(BF16)
        wbb_ref[...] = wb_ref[0].astype(BF16)
        wcb_ref[...] = wc_ref[0].astype(BF16)

    d = lambda y, w: jnp.dot(y[...], w[...], preferred_element_type=F32)
    acc = ga_ref[...].astype(F32) * d(ya_ref, wab_ref)
    acc += gb_ref[...].astype(F32) * d(yb_ref, wbb_ref)
    acc += gc_ref[...].astype(F32) * d(yc_ref, wcb_ref)
    o_ref[...] = acc.astype(o_ref.dtype)


def _merge(ya, yb, yc, gates, wa, wb, wc, layer, tm, tn):
    n = ya.shape[0]
    d = wa.shape[2]
    tm = min(tm, n)
    nj = d // tn
    y_spec = lambda a: pl.BlockSpec((tm, a.shape[1]), lambda j, i: (i, 0))
    w_spec = lambda a: pl.BlockSpec((1, a.shape[1], tn), lambda j, i: (layer, 0, j),
                                    pipeline_mode=pl.Buffered(1))
    g_spec = lambda br: pl.BlockSpec((tm, tn), lambda j, i: (i, br * nj + j))
    return pl.pallas_call(
        _merge_kernel,
        out_shape=jax.ShapeDtypeStruct((n, d), BF16),
        grid=(nj, n // tm),
        in_specs=[y_spec(ya), y_spec(yb), y_spec(yc), g_spec(0), g_spec(1), g_spec(2),
                  w_spec(wa), w_spec(wb), w_spec(wc)],
        out_specs=pl.BlockSpec((tm, tn), lambda j, i: (i, j)),
        scratch_shapes=[pltpu.VMEM((a.shape[1], tn), BF16) for a in (wa, wb, wc)],
        compiler_params=_params("arbitrary", "arbitrary"),
        name="merge",
    )(ya, yb, yc, gates, gates, gates, wa, wb, wc)


def _out_kernel(m_ref, w_ref, x_ref, g_ref, *rest):
    gn_ref, o_ref, hn_ref, wb_ref = rest if len(rest) == 4 else (None, rest[0], None, rest[1])

    @pl.when(pl.program_id(0) == 0)
    def _():
        wb_ref[...] = w_ref[0].astype(BF16)

    o = jnp.dot(m_ref[...], wb_ref[...], preferred_element_type=F32)
    ms = jnp.mean(o * o, axis=-1, keepdims=True)
    xn = x_ref[...] + o * lax.rsqrt(ms + NORM_EPS) * g_ref[...]
    o_ref[...] = xn
    if hn_ref is not None:
        ms_n = jnp.mean(xn * xn, axis=-1, keepdims=True)
        hn_ref[...] = (xn * lax.rsqrt(ms_n + NORM_EPS) * gn_ref[...]).astype(hn_ref.dtype)


def _out_proj(merged, w_out, layer, x2d, g_post, g_next, tm):
    n, d = x2d.shape
    tm = min(tm, n)
    tok = pl.BlockSpec((tm, d), lambda i: (i, 0))
    vec = pl.BlockSpec((1, d), lambda i: (0, 0))
    nxt = g_next is not None
    out = pl.pallas_call(
        _out_kernel,
        out_shape=[jax.ShapeDtypeStruct((n, d), F32)] + [jax.ShapeDtypeStruct((n, d), BF16)] * nxt,
        grid=(n // tm,),
        in_specs=[tok, pl.BlockSpec((1, d, d), lambda i: (layer, 0, 0), pipeline_mode=pl.Buffered(1)),
                  tok, vec] + [vec] * nxt,
        out_specs=[tok] + [tok] * nxt,
        scratch_shapes=[pltpu.VMEM((d, d), BF16)],
        compiler_params=_params("arbitrary"),
        name="out_proj",
    )(merged, w_out, x2d, g_post.reshape(1, d), *([g_next.reshape(1, d)] if nxt else []))
    return (out[0], out[1]) if nxt else (out[0], None)


def kernel(x, mem, g_pre, w_in, mu_shift, decay_base, decay_up, iclr_base, iclr_up, k_k, k_a, r_k,
           gn_w, gn_b, attn_sinks, g_mem, w_mem_kv, w_up_rwkv, w_up_swa, w_up_xattn, w_out, g_post):
    batch, seq_len, d = x.shape
    n = batch * seq_len
    x2d = x.reshape(n, d)
    mem2d = mem.reshape(batch * mem.shape[1], d)
    w = RWKV_WIDTH
    kvh = SWA_KV_HEADS * SWA_HEAD
    o_gate_a = SHIFT_COLS
    o_q_b = o_gate_a + w
    o_kv_b = o_q_b + SWA_WIDTH
    o_gate_b = o_kv_b + 2 * kvh
    o_q_c = o_gate_b + SWA_WIDTH
    o_gate_c = o_q_c + XA_WIDTH
    o_merge = o_gate_c + XA_WIDTH

    tn = 1024
    blocks = lambda off, width: [off + t * tn for t in range(width // tn)]
    for l in range(DEPTH):
        wda = jnp.zeros((LANES, 2 * w), F32)
        wda = wda.at[:DECAY_RANK, :w].set(decay_up[l]).at[DECAY_RANK:, w:].set(iclr_up[l])

        if l == 0:
            h = _rmsnorm(x2d, g_pre[l], 512)
        p_rkv = _proj(h, w_in, l, blocks(0, 3 * w), tn, out_dtype=F32, tm=1024, name="proj_rkv")
        p_qc, p_kv, p_da = _proj_small(h, w_in, l, o_q_c, XA_HEAD ** -0.5, o_kv_b, 3 * w, 1024)
        p_gq, prep = _proj_prep(h, w_in, l, [o_gate_a, o_gate_b, o_gate_c, o_q_b], tn,
                                [1.0, 1.0, 1.0, SWA_HEAD ** -0.5], 3, p_rkv, p_da, seq_len, mu_shift[l], wda,
                                decay_base[l], iclr_base[l], k_k[l], k_a[l], r_k[l].reshape(-1), 1024)
        tm_m = min(1024, n)
        ncol_m = (n // BLOCK) // (n // tm_m)
        tn_m = N_BRANCH * d // ncol_m
        p_merge, y_b = _proj_swa(h, w_in, l, [o_merge + t * tn_m for t in range(ncol_m)], tn_m, "sigmoid",
                                 p_gq, 3 * w, p_kv, 0, 2 * kvh, p_gq, w, attn_sinks[l], seq_len, tm_m)

        y_a = _rwkv_chunk(prep, p_gq, 0, gn_w[l], gn_b[l], batch, seq_len)

        mem_n = _rmsnorm(mem2d, g_mem[l], 512)
        kv_c = _proj(mem_n, w_mem_kv, l, blocks(0, 2 * XA_WIDTH), tn, out_dtype=BF16, tm=1024,
                     name="mem_kv")
        y_c = _xattn(p_qc, 0, kv_c, p_gq, w + SWA_WIDTH, batch, seq_len, 512)

        merged = _merge(y_a, y_b, y_c, p_merge, w_up_rwkv, w_up_swa, w_up_xattn, l, 1024, 1024)
        x2d, h = _out_proj(merged, w_out, l, x2d, g_post[l], g_pre[l + 1] if l + 1 < DEPTH else None, 512)
    return x2d.reshape(batch, seq_len, d)
```
